```python
import math
import jax
import jax.numpy as jnp
from jax import lax
import numpy as np

D_MODEL = 1024
BATCH = 16
SEQ = 2048
DEPTH = 2
DEC_BATCH = 128
DEC_SEQ = 4
PAST_LEN = 8192
PAGE_SIZE = 128

N_ATTN_LAYERS = (DEPTH + 1) // 2
N_CONV_LAYERS = DEPTH // 2
NORM_EPS = 1e-6

ROPE_THETA = 500000.0
ATTN_BLOCK = 128

DIFF_HEADS = 4
DIFF_D = 64
DIFF_V = 2 * DIFF_D
ROT_DIFF = DIFF_D // 4
DQ_W = 2 * DIFF_HEADS * DIFF_D
DV_W = DIFF_HEADS * DIFF_V

MLA_HEADS = 8
Q_LORA = 384
KV_LORA = 256
QK_NOPE = 64
QK_ROPE = 32
V_HEAD = 64
MLA_CACHE_W = KV_LORA + QK_ROPE

ATTN_IN = 2 * DQ_W + DV_W + Q_LORA + KV_LORA + QK_ROPE
ATTN_SPLITS = (DQ_W, 2 * DQ_W, 2 * DQ_W + DV_W, 2 * DQ_W + DV_W + Q_LORA, 2 * DQ_W + DV_W + Q_LORA + KV_LORA)
ATTN_MIX_W = DV_W + MLA_HEADS * V_HEAD

CONV_CH = 512
CONV_W = 31
LN_EPS = 1e-5

RWKV_HEADS = 8
RWKV_N = 64
RWKV_W = RWKV_HEADS * RWKV_N
DECAY_LORA = 32
AAA_LORA = 32
GATE_LORA = 96
RWKV_IN = 3 * RWKV_W + DECAY_LORA + AAA_LORA + GATE_LORA
RWKV_SPLITS = (RWKV_W, 2 * RWKV_W, 3 * RWKV_W, 3 * RWKV_W + DECAY_LORA, 3 * RWKV_W + DECAY_LORA + AAA_LORA)
GN_EPS = 64e-5

MIX_IN = 2 * CONV_CH + RWKV_IN
CONV_MIX_W = CONV_CH + RWKV_W

PEER_HEADS = 8
PEER_KEYS = 128
PEER_EXPERTS = PEER_KEYS * PEER_KEYS
PEER_DKEY = 256
PEER_TOPK = 16
PEER_CHUNK = 256

kernel_name = 'hybrid_diffattn_mla_conformer_rwkv7_peer_step'


def rms_norm(x, g):
    xf = x.astype(jnp.float32)
    y = xf * lax.rsqrt(jnp.mean(xf * xf, axis=-1, keepdims=True) + NORM_EPS)
    return (y * g.astype(jnp.float32)).astype(x.dtype)


def layer_norm(x, g, b):
    xf = x.astype(jnp.float32)
    mu = jnp.mean(xf, axis=-1, keepdims=True)
    var = jnp.mean(jnp.square(xf - mu), axis=-1, keepdims=True)
    return ((xf - mu) * lax.rsqrt(var + LN_EPS) * g.astype(jnp.float32) + b.astype(jnp.float32)).astype(x.dtype)


def rotary(x, pos, rot_dim):
    half = rot_dim // 2
    inv = ROPE_THETA ** (-jnp.arange(half, dtype=jnp.float32) / half)
    ang = pos.astype(jnp.float32)[:, None] * inv[None, :]
    cos = jnp.cos(ang)[:, None, :]
    sin = jnp.sin(ang)[:, None, :]
    xf = x.astype(jnp.float32)
    x1 = xf[..., :half]
    x2 = xf[..., half:rot_dim]
    out = jnp.concatenate([x1 * cos - x2 * sin, x2 * cos + x1 * sin, xf[..., rot_dim:]], axis=-1)
    return out.astype(x.dtype)


def _block_stats(q, k, v, mask):
    s = jnp.einsum('bqgrd,bkgd->bgrqk', q, k, preferred_element_type=jnp.float32)
    if mask is not None:
        s = jnp.where(mask, s, -jnp.inf)
    m = jnp.max(s, axis=-1)
    p = jnp.exp(s - m[..., None])
    l = jnp.sum(p, axis=-1)
    acc = jnp.einsum('bgrqk,bkge->bgrqe', p, v.astype(jnp.float32))
    return m, l, acc


def _merge(a, b):
    m_a, l_a, acc_a = a
    m_b, l_b, acc_b = b
    m = jnp.maximum(m_a, m_b)
    ca = jnp.exp(m_a - m)
    cb = jnp.exp(m_b - m)
    return m, l_a * ca + l_b * cb, acc_a * ca[..., None] + acc_b * cb[..., None]


def causal_attention(q, k, v):
    b, s, g, r, dk = q.shape
    e = v.shape[-1]
    blk = ATTN_BLOCK if s % ATTN_BLOCK == 0 else s
    nb = s // blk
    q_blocks = jnp.swapaxes(q.reshape(b, nb, blk, g, r, dk), 0, 1)
    k_pos = jnp.arange(s)

    def one_block(args):
        bi, qb = args
        q_pos = bi * blk + jnp.arange(blk)
        _, l, acc = _block_stats(qb, k, v, k_pos[None, :] <= q_pos[:, None])
        return acc / l[..., None]

    out = lax.map(one_block, (jnp.arange(nb), q_blocks))
    return jnp.moveaxis(out, 0, 3).reshape(b, g, r, s, e)


def paged_attention(q, k_new, v_new, gather, page_table):
    t = q.shape[1]
    causal = jnp.arange(t)[None, :] <= jnp.arange(t)[:, None]

    def step(carry, pages):
        kp, vp = gather(pages)
        return _merge(carry, _block_stats(q, kp, vp, None)), None

    (_, l, acc), _ = lax.scan(step, _block_stats(q, k_new, v_new, causal), page_table.T)
    return acc / l[..., None]


def attn_mixer(h, pos, lam_init, prm, past):
    (norm_g, w_in, lam_p, subln_g, qn_g, kvn_g, w_uq, w_uk, w_uv, w_out) = prm
    b, t, _ = h.shape
    z = rms_norm(h, norm_g) @ w_in
    dq, dk, dv, cq, ckv, kr = jnp.split(z, ATTN_SPLITS, axis=-1)
    dq = rotary(dq.reshape(b, t, 2 * DIFF_HEADS, DIFF_D), pos, ROT_DIFF) * (DIFF_D ** -0.5)
    dk = rotary(dk.reshape(b, t, 2 * DIFF_HEADS, DIFF_D), pos, ROT_DIFF)
    dv = dv.reshape(b, t, DIFF_HEADS, DIFF_V)
    dv_rep = jnp.repeat(dv, 2, axis=2)
    q = (rms_norm(cq, qn_g) @ w_uq).reshape(b, t, MLA_HEADS, QK_NOPE + QK_ROPE)
    q_lat = jnp.einsum('bthn,hcn->bthc', q[..., :QK_NOPE], w_uk)
    q_rot = rotary(q[..., QK_NOPE:], pos, QK_ROPE)
    mq = jnp.concatenate([q_lat, q_rot], axis=-1) * ((QK_NOPE + QK_ROPE) ** -0.5)
    mkv = jnp.concatenate([rms_norm(ckv, kvn_g), rotary(kr[:, :, None, :], pos, QK_ROPE)[:, :, 0]], axis=-1)
    mk = mkv[:, :, None, :]
    mv = mkv[:, :, None, :KV_LORA]
    if past is None:
        od = causal_attention(dq[:, :, :, None, :], dk, dv_rep)
        om = causal_attention(mq[:, :, None], mk, mv)
    else:
        k_pool, v_pool, m_pool, li, page_table = past

        def gather_diff(pages):
            return k_pool[li, pages], jnp.repeat(v_pool[li, pages], 2, axis=2)

        def gather_mla(pages):
            c = m_pool[li, pages][:, :, None, :]
            return c, c[..., :KV_LORA]

        od = paged_attention(dq[:, :, :, None, :], dk, dv_rep, gather_diff, page_table)
        om = paged_attention(mq[:, :, None], mk, mv, gather_mla, page_table)
    lp = lam_p.astype(jnp.float32)
    lam = jnp.exp(jnp.sum(lp[0] * lp[1])) - jnp.exp(jnp.sum(lp[2] * lp[3])) + lam_init
    od = od.reshape(b, DIFF_HEADS, 2, t, DIFF_V)
    od = od[:, :, 0] - lam * od[:, :, 1]
    od = rms_norm(od, subln_g) * (1.0 - lam_init)
    od = jnp.transpose(od, (0, 2, 1, 3)).reshape(b, t, DV_W)
    om = jnp.einsum('bhtc,hcv->bthv', om[:, 0], w_uv).reshape(b, t, MLA_HEADS * V_HEAD)
    y = jnp.concatenate([od, om], axis=-1).astype(h.dtype) @ w_out
    return y, dk, dv, mkv


def wkv_scan(s0, r, w, k, v, kk, a):
    def step(s, inp):
        r_t, w_t, k_t, v_t, kk_t, a_t = inp
        sk = jnp.einsum('bhij,bhj->bhi', s, kk_t)
        s = s * w_t[:, :, None, :] - sk[..., None] * (kk_t * a_t)[:, :, None, :] + v_t[..., None] * k_t[:, :, None, :]
        return s, jnp.einsum('bhij,bhj->bhi', s, r_t)

    xs = tuple(jnp.moveaxis(u, 1, 0) for u in (r, w, k, v, kk, a))
    s, ys = lax.scan(step, s0.astype(jnp.float32), xs)
    return jnp.moveaxis(ys, 0, 1), s


def conv_rwkv_mixer(h, prm, conv_buf, shift_prev, wkv0):
    (norm_g, w_in, cw, cb, lng, lnb, mu, w0, w2, a0, a2, g2, k_k, k_a, r_k, gn_g, gn_b, w_out) = prm
    b, t, _ = h.shape
    z = rms_norm(h, norm_g) @ w_in
    ga, gb, zr = jnp.split(z, (CONV_CH, 2 * CONV_CH), axis=-1)
    u = ga * jax.nn.sigmoid(gb)
    full = jnp.concatenate([conv_buf.astype(u.dtype), u], axis=1)
    c = lax.conv_general_dilated(full, cw[:, None, :].astype(full.dtype), (1,), 'VALID',
                                 dimension_numbers=('NWC', 'WIO', 'NWC'),
                                 feature_group_count=CONV_CH) + cb
    c = jax.nn.silu(layer_norm(c, lng, lnb))
    new_buf = full[:, full.shape[1] - (CONV_W - 1):]
    prev = jnp.concatenate([shift_prev[:, None].astype(zr.dtype), zr[:, :-1]], axis=1)
    zs = zr + (prev - zr) * mu
    r, k, v, wl, al, gl = jnp.split(zs, RWKV_SPLITS, axis=-1)

    def heads(u_):
        return u_.astype(jnp.float32).reshape(b, t, RWKV_HEADS, RWKV_N)

    wlog = -jax.nn.softplus(-(w0 + jnp.tanh(wl) @ w2)) - 0.5
    decay = jnp.exp(-jnp.exp(wlog.astype(jnp.float32)))
    a = jax.nn.sigmoid(a0 + al @ a2)
    g = jax.nn.sigmoid(gl) @ g2
    kk = heads(k * k_k)
    kk = kk / jnp.maximum(jnp.sqrt(jnp.sum(kk * kk, axis=-1, keepdims=True)), 1e-12)
    k = k * (1.0 + (a - 1.0) * k_a)
    rh, kh, vh, ah = heads(r), heads(k), heads(v), heads(a)
    o, wkv = wkv_scan(wkv0, rh, heads(decay), kh, vh, kk, ah)
    mean = jnp.mean(o, axis=-1, keepdims=True)
    var = jnp.mean(jnp.square(o - mean), axis=-1, keepdims=True)
    o = ((o - mean) * lax.rsqrt(var + GN_EPS)).reshape(b, t, RWKV_W) * gn_g + gn_b
    bonus = jnp.sum(rh * kh * r_k.astype(jnp.float32), axis=-1, keepdims=True) * vh
    d = (o + bonus.reshape(b, t, RWKV_W)) * g
    y = jnp.concatenate([c, d.astype(c.dtype)], axis=-1) @ w_out
    return y, new_buf, zr[:, -1], wkv


def peer_ffn(h, prm):
    norm_g, w_q, keys, u_tab, v_tab = prm
    b, t, d = h.shape
    n = b * t
    chunk = math.gcd(n, PEER_CHUNK)
    x = rms_norm(h, norm_g).reshape(n // chunk, chunk, d)

    def one_chunk(xc):
        c = xc.shape[0]
        q = (xc @ w_q).reshape(c, PEER_HEADS, 2, PEER_DKEY // 2)
        s = jnp.einsum('chpd,hpkd->chpk', q, keys, preferred_element_type=jnp.float32)
        sv, si = lax.top_k(s, PEER_TOPK)
        cand = (sv[:, :, 0, :, None] + sv[:, :, 1, None, :]).reshape(c, PEER_HEADS, PEER_TOPK * PEER_TOPK)
        cidx = (si[:, :, 0, :, None] * PEER_KEYS + si[:, :, 1, None, :]).reshape(c, PEER_HEADS, PEER_TOPK * PEER_TOPK)
        best, sel = lax.top_k(cand, PEER_TOPK)
        idx = jnp.take_along_axis(cidx, sel, axis=-1)
        gate = jax.nn.softmax(best, axis=-1)
        act = jax.nn.gelu(jnp.einsum('cd,chkd->chk', xc, u_tab[idx], preferred_element_type=jnp.float32))
        return jnp.einsum('chk,chkd->cd', (gate * act).astype(xc.dtype), v_tab[idx])

    return lax.map(one_chunk, x).reshape(b, t, d).astype(h.dtype)


def setup_inputs(seed: int = 0) -> dict:
    key = jax.random.key(seed)
    ks = iter(jax.random.split(key, 64))

    def nrm(shape, scale):
        return jax.random.normal(next(ks), shape, jnp.float32) * scale

    def gain(shape):
        return 1.0 + nrm(shape, 0.02)

    na, nc, nl = N_ATTN_LAYERS, N_CONV_LAYERS, DEPTH
    n_pages = PAST_LEN // PAGE_SIZE
    n_used = DEC_BATCH * n_pages
    n_pool = (5 * n_used + 3) // 4
    page_table = jax.random.permutation(next(ks), n_pool)[:n_used].reshape(DEC_BATCH, n_pages).astype(jnp.int32)
    return {
        'x_prompt': nrm((BATCH, SEQ, D_MODEL), 1.0),
        'x_sample': nrm((DEC_BATCH, DEC_SEQ, D_MODEL), 1.0),
        'cache_diff_k': nrm((na, n_pool, PAGE_SIZE, 2 * DIFF_HEADS, DIFF_D), 1.0),
        'cache_diff_v': nrm((na, n_pool, PAGE_SIZE, DIFF_HEADS, DIFF_V), 1.0),
        'cache_mla': nrm((na, n_pool, PAGE_SIZE, MLA_CACHE_W), 1.0),
        'page_table': page_table,
        'state_conv': nrm((nc, DEC_BATCH, CONV_W - 1, CONV_CH), 0.5),
        'state_shift': nrm((nc, DEC_BATCH, RWKV_IN), 1.0),
        'state_wkv': nrm((nc, DEC_BATCH, RWKV_HEADS, RWKV_N, RWKV_N), 0.3),
        'attn_norm': gain((na, D_MODEL)),
        'attn_w_in': nrm((na, D_MODEL, ATTN_IN), D_MODEL ** -0.5),
        'diff_lambda': nrm((na, 4, DIFF_D), 0.1),
        'diff_subln': gain((na, DIFF_V)),
        'mla_q_norm': gain((na, Q_LORA)),
        'mla_kv_norm': gain((na, KV_LORA)),
        'mla_w_uq': nrm((na, Q_LORA, MLA_HEADS * (QK_NOPE + QK_ROPE)), Q_LORA ** -0.5),
        'mla_w_uk': nrm((na, MLA_HEADS, KV_LORA, QK_NOPE), KV_LORA ** -0.5),
        'mla_w_uv': nrm((na, MLA_HEADS, KV_LORA, V_HEAD), KV_LORA ** -0.5),
        'attn_w_out': nrm((na, ATTN_MIX_W, D_MODEL), ATTN_MIX_W ** -0.5),
        'mix_norm': gain((nc, D_MODEL)),
        'mix_w_in': nrm((nc, D_MODEL, MIX_IN), D_MODEL ** -0.5),
        'conv_w': nrm((nc, CONV_W, CONV_CH), CONV_W ** -0.5),
        'conv_b': nrm((nc, CONV_CH), 0.02),
        'conv_ln_g': gain((nc, CONV_CH)),
        'conv_ln_b': nrm((nc, CONV_CH), 0.02),
        'rwkv_mu': jax.random.uniform(next(ks), (nc, RWKV_IN), jnp.float32),
        'rwkv_w0': nrm((nc, RWKV_W), 0.5),
        'rwkv_w2': nrm((nc, DECAY_LORA, RWKV_W), 0.1),
        'rwkv_a0': nrm((nc, RWKV_W), 0.5),
        'rwkv_a2': nrm((nc, AAA_LORA, RWKV_W), 0.1),
        'rwkv_g2': nrm((nc, GATE_LORA, RWKV_W), GATE_LORA ** -0.5),
        'rwkv_k_k': 0.85 + nrm((nc, RWKV_W), 0.02),
        'rwkv_k_a': 1.0 + nrm((nc, RWKV_W), 0.02),
        'rwkv_r_k': nrm((nc, RWKV_HEADS, RWKV_N), 0.1),
        'rwkv_ln_g': gain((nc, RWKV_W)),
        'rwkv_ln_b': nrm((nc, RWKV_W), 0.02),
        'mix_w_out': nrm((nc, CONV_MIX_W, D_MODEL), CONV_MIX_W ** -0.5),
        'ffn_norm': gain((nl, D_MODEL)),
        'peer_w_q': nrm((nl, D_MODEL, PEER_HEADS * PEER_DKEY), D_MODEL ** -0.5),
        'peer_keys': nrm((nl, PEER_HEADS, 2, PEER_KEYS, PEER_DKEY // 2), (PEER_DKEY // 2) ** -0.5),
        'peer_u': nrm((nl, PEER_EXPERTS, D_MODEL), D_MODEL ** -0.5),
        'peer_v': nrm((nl, PEER_EXPERTS, D_MODEL), 0.1),
        'final_norm': gain((D_MODEL,)),
    }


def reference(x_prompt, x_sample, cache_diff_k, cache_diff_v, cache_mla, page_table,
              state_conv, state_shift, state_wkv,
              attn_norm, attn_w_in, diff_lambda, diff_subln, mla_q_norm, mla_kv_norm,
              mla_w_uq, mla_w_uk, mla_w_uv, attn_w_out,
              mix_norm, mix_w_in, conv_w, conv_b, conv_ln_g, conv_ln_b,
              rwkv_mu, rwkv_w0, rwkv_w2, rwkv_a0, rwkv_a2, rwkv_g2,
              rwkv_k_k, rwkv_k_a, rwkv_r_k, rwkv_ln_g, rwkv_ln_b, mix_w_out,
              ffn_norm, peer_w_q, peer_keys, peer_u, peer_v, final_norm):
    bp, sp, _ = x_prompt.shape
    ss = x_sample.shape[1]
    past_len = page_table.shape[1] * cache_mla.shape[2]
    pos_p = jnp.arange(sp)
    pos_s = past_len + jnp.arange(ss)
    h_p, h_s = x_prompt, x_sample
    dk_p, dv_p, mla_p, dk_s, dv_s, mla_s = [], [], [], [], [], []
    conv_p, shift_p, wkv_p, conv_s, shift_s, wkv_s = [], [], [], [], [], []
    for layer in range(DEPTH):
        i = layer // 2
        if layer % 2 == 0:
            lam_init = 0.8 - 0.6 * math.exp(-0.3 * layer)
            prm = (attn_norm[i], attn_w_in[i], diff_lambda[i], diff_subln[i], mla_q_norm[i],
                   mla_kv_norm[i], mla_w_uq[i], mla_w_uk[i], mla_w_uv[i], attn_w_out[i])
            y_p, k_n, v_n, c_n = attn_mixer(h_p, pos_p, lam_init, prm, None)
            dk_p.append(k_n)
            dv_p.append(v_n)
            mla_p.append(c_n)
            y_s, k_n, v_n, c_n = attn_mixer(h_s, pos_s, lam_init, prm,
                                            (cache_diff_k, cache_diff_v, cache_mla, i, page_table))
            dk_s.append(k_n)
            dv_s.append(v_n)
            mla_s.append(c_n)
        else:
            prm = (mix_norm[i], mix_w_in[i], conv_w[i], conv_b[i], conv_ln_g[i], conv_ln_b[i],
                   rwkv_mu[i], rwkv_w0[i], rwkv_w2[i], rwkv_a0[i], rwkv_a2[i], rwkv_g2[i],
                   rwkv_k_k[i], rwkv_k_a[i], rwkv_r_k[i], rwkv_ln_g[i], rwkv_ln_b[i], mix_w_out[i])
            y_p, cb_n, sh_n, st_n = conv_rwkv_mixer(
                h_p, prm,
                jnp.zeros((bp, CONV_W - 1, CONV_CH), h_p.dtype),
                jnp.zeros((bp, RWKV_IN), h_p.dtype),
                jnp.zeros((bp, RWKV_HEADS, RWKV_N, RWKV_N), jnp.float32))
            conv_p.append(cb_n)
            shift_p.append(sh_n)
            wkv_p.append(st_n)
            y_s, cb_n, sh_n, st_n = conv_rwkv_mixer(h_s, prm, state_conv[i], state_shift[i], state_wkv[i])
            conv_s.append(cb_n)
            shift_s.append(sh_n)
            wkv_s.append(st_n)
        h_p = h_p + y_p
        h_s = h_s + y_s
        fprm = (ffn_norm[layer], peer_w_q[layer], peer_keys[layer], peer_u[layer], peer_v[layer])
        h_p = h_p + peer_ffn(h_p, fprm)
        h_s = h_s + peer_ffn(h_s, fprm)
    y_prompt = rms_norm(h_p, final_norm)
    y_sample = rms_norm(h_s, final_norm)
    new_diff_k_prompt = jnp.stack(dk_p)
    new_diff_v_prompt = jnp.stack(dv_p)
    new_mla_prompt = jnp.stack(mla_p)
    new_conv_prompt = jnp.stack(conv_p)
    new_shift_prompt = jnp.stack(shift_p)
    new_wkv_prompt = jnp.stack(wkv_p)
    new_diff_k_sample = jnp.stack(dk_s)
    new_diff_v_sample = jnp.stack(dv_s)
    new_mla_sample = jnp.stack(mla_s)
    new_conv_sample = jnp.stack(conv_s)
    new_shift_sample = jnp.stack(shift_s)
    new_wkv_sample = jnp.stack(wkv_s)
    return (y_prompt, y_sample,
            new_diff_k_prompt, new_diff_v_prompt, new_mla_prompt,
            new_conv_prompt, new_shift_prompt, new_wkv_prompt,
            new_diff_k_sample, new_diff_v_sample, new_mla_sample,
            new_conv_sample, new_shift_sample, new_wkv_sample)
```

```python
import functools
import math

import numpy as np
import jax
import jax.numpy as jnp
from jax import lax
from jax.experimental import pallas as pl
from jax.experimental.pallas import tpu as pltpu

F32 = jnp.float32
BF16 = jnp.bfloat16

NORM_EPS = 1e-6
LN_EPS = 1e-5
GN_EPS = 64e-5
ROPE_THETA = 500000.0

DIFF_HEADS = 4
DIFF_D = 64
DIFF_V = 128
DQ_W = 2 * DIFF_HEADS * DIFF_D
DV_W = DIFF_HEADS * DIFF_V
MLA_HEADS = 8
Q_LORA = 384
KV_LORA = 256
QK_NOPE = 64
QK_ROPE = 32
V_HEAD = 64
MLA_CACHE_W = KV_LORA + QK_ROPE
MLA_PAD_W = 384
ATTN_IN = 2 * DQ_W + DV_W + Q_LORA + KV_LORA + QK_ROPE
ATTN_IN_PAD = 2304
CONV_CH = 512
CONV_W = 31
RWKV_HEADS = 8
RWKV_N = 64
RWKV_W = 512
LORA_W = 160
LORA_PAD = 256
MIX_IN_PAD = 2 * CONV_CH + 3 * RWKV_W + LORA_PAD
PEER_HEADS = 8
PEER_KEYS = 128
PEER_TOPK = 16
LANES = 128
VMEM_LIMIT_MB = 56

DIFF_SCALE = DIFF_D ** -0.5
MLA_SCALE = (QK_NOPE + QK_ROPE) ** -0.5
NEG_INF = float("-inf")


def _cparams(sem, vmem_mb=VMEM_LIMIT_MB):
    return pltpu.CompilerParams(dimension_semantics=sem, vmem_limit_bytes=vmem_mb * 1024 * 1024)


def _rms(x, g, eps=NORM_EPS):
    return x * lax.rsqrt(jnp.mean(x * x, axis=-1, keepdims=True) + eps) * g


def _nt(a, b):
    return lax.dot_general(a, b, (((1,), (1,)), ((), ())), preferred_element_type=F32)


def _mm(a, b):
    return jnp.dot(a, b, preferred_element_type=F32)


def _rope_tables(pos):
    posf = pos.astype(F32)[:, None]
    inv8 = ROPE_THETA ** (-jnp.arange(8, dtype=F32) / 8)
    inv16 = ROPE_THETA ** (-jnp.arange(16, dtype=F32) / 16)
    a8 = posf * inv8[None, :]
    a16 = posf * inv16[None, :]
    c8, s8, c16, s16 = jnp.cos(a8), jnp.sin(a8), jnp.cos(a16), jnp.sin(a16)
    lane = np.arange(LANES)
    d = lane % DIFF_D
    i8 = d % 8
    i16 = lane % 16
    cd = jnp.where((d < 16)[None, :], c8[:, i8], 1.0)
    sa = jnp.where((d < 8)[None, :], -s8[:, i8], 0.0)
    sb = jnp.where(((d >= 8) & (d < 16))[None, :], s8[:, i8], 0.0)
    cm = c16[:, i16]
    sm = s16[:, i16]
    ck = jnp.where((lane < 32)[None, :], c16[:, i16], 0.0)
    ska = jnp.where((lane < 16)[None, :], -s16[:, i16], 0.0)
    skb = jnp.where(((lane >= 16) & (lane < 32))[None, :], s16[:, i16], 0.0)
    return jnp.stack([cd, sa, sb, cm, sm, ck, ska, skb]).astype(F32)


def _attn_in_kernel(h_ref, ng_ref, win_ref, tab_ref, qng_ref, kvng_ref, wuq_ref, wuk_ref, sel_ref,
                    dq_ref, dk_ref, dkb_ref, dv_ref, dvb_ref, mq_ref, mkv_ref, mkb_ref):
    xn = _rms(h_ref[...], ng_ref[...]).astype(BF16)
    z = _mm(xn, win_ref[...])
    cd, sa, sb = tab_ref[0], tab_ref[1], tab_ref[2]

    def rot(x):
        return x * cd + pltpu.roll(x, LANES - 8, 1) * sa + pltpu.roll(x, 8, 1) * sb

    for j in range(DQ_W // LANES):
        sl = slice(j * LANES, (j + 1) * LANES)
        dq_ref[:, sl] = (rot(z[:, sl]) * DIFF_SCALE).astype(BF16)
        k = rot(z[:, DQ_W + j * LANES:DQ_W + (j + 1) * LANES])
        dk_ref[:, sl] = k
        dkb_ref[:, sl] = k.astype(BF16)
    dv = z[:, 2 * DQ_W:2 * DQ_W + DV_W]
    dv_ref[...] = dv
    dvb_ref[...] = dv.astype(BF16)

    c0 = 2 * DQ_W + DV_W
    cqn = _rms(z[:, c0:c0 + Q_LORA], qng_ref[...]).astype(BF16)
    q = _mm(cqn, wuq_ref[...])
    cm, sm = tab_ref[3], tab_ref[4]
    x1 = q[:, 512:640]
    x2 = q[:, 640:768]
    r1 = (x1 * cm - x2 * sm) * MLA_SCALE
    r2 = (x2 * cm + x1 * sm) * MLA_SCALE
    rcat = jnp.concatenate([r1, r2], axis=1).astype(BF16)
    qn = q[:, :512].astype(BF16)
    for hh in range(MLA_HEADS):
        slab = qn[:, (hh // 2) * LANES:(hh // 2 + 1) * LANES]
        lat = _mm(slab, wuk_ref[hh]) * MLA_SCALE
        rt = _mm(rcat, sel_ref[hh])
        mq_ref[hh, :, 0:KV_LORA] = lat.astype(BF16)
        mq_ref[hh, :, KV_LORA:MLA_PAD_W] = rt.astype(BF16)

    c1 = c0 + Q_LORA
    ckvn = _rms(z[:, c1:c1 + KV_LORA], kvng_ref[...])
    ck, ska, skb = tab_ref[5], tab_ref[6], tab_ref[7]
    slab = z[:, c1 + KV_LORA:c1 + KV_LORA + LANES]
    kr = slab * ck + pltpu.roll(slab, LANES - 16, 1) * ska + pltpu.roll(slab, 16, 1) * skb
    mkv_ref[:, 0:KV_LORA] = ckvn
    mkv_ref[:, KV_LORA:MLA_CACHE_W] = kr[:, 0:QK_ROPE]
    mkb_ref[:, 0:KV_LORA] = ckvn.astype(BF16)
    mkb_ref[:, KV_LORA:MLA_PAD_W] = kr.astype(BF16)


def _attn_in_weights(w_in, w_uq, w_uk):
    d = w_in.shape[0]
    win = jnp.concatenate([w_in, jnp.zeros((d, ATTN_IN_PAD - ATTN_IN), w_in.dtype)], axis=1).astype(BF16)
    perm = np.zeros((MLA_HEADS * (QK_NOPE + QK_ROPE),), np.int32)
    for hh in range(MLA_HEADS):
        base = hh * (QK_NOPE + QK_ROPE)
        perm[hh * 64:(hh + 1) * 64] = base + np.arange(64)
        perm[512 + hh * 16:512 + (hh + 1) * 16] = base + 64 + np.arange(16)
        perm[640 + hh * 16:640 + (hh + 1) * 16] = base + 80 + np.arange(16)
    wuq = w_uq[:, perm].astype(BF16)
    wukt = jnp.swapaxes(w_uk, 1, 2)
    zeros = jnp.zeros_like(wukt)
    even = jnp.concatenate([wukt, zeros], axis=1)
    odd = jnp.concatenate([zeros, wukt], axis=1)
    is_even = (np.arange(MLA_HEADS) % 2 == 0)[:, None, None]
    wuk = jnp.where(is_even, even, odd).astype(BF16)
    sel = np.zeros((MLA_HEADS, 256, LANES), np.float32)
    for hh in range(MLA_HEADS):
        for i in range(16):
            sel[hh, hh * 16 + i, i] = 1.0
            sel[hh, 128 + hh * 16 + i, 16 + i] = 1.0
    return win, wuq, wuk, jnp.asarray(sel, BF16)


def _attn_in(h, tabs, n_tab_tiles, norm_g, win, qn_g, kvn_g, wuq, wuk, sel, tm):
    n, d = h.shape
    full = lambda *shape: pl.BlockSpec(shape, lambda i: (0,) * len(shape))
    row = lambda w: pl.BlockSpec((tm, w), lambda i: (i, 0))
    outs = (
        jax.ShapeDtypeStruct((n, DQ_W), BF16), jax.ShapeDtypeStruct((n, DQ_W), F32), jax.ShapeDtypeStruct((n, DQ_W), BF16),
        jax.ShapeDtypeStruct((n, DV_W), F32), jax.ShapeDtypeStruct((n, DV_W), BF16),
        jax.ShapeDtypeStruct((MLA_HEADS, n, MLA_PAD_W), BF16),
        jax.ShapeDtypeStruct((n, MLA_CACHE_W), F32), jax.ShapeDtypeStruct((n, MLA_PAD_W), BF16),
    )
    return pl.pallas_call(
        _attn_in_kernel,
        grid=(n // tm,),
        in_specs=[row(d), full(1, d), full(d, ATTN_IN_PAD),
                  pl.BlockSpec((8, tm, LANES), lambda i: (0, i % n_tab_tiles, 0)),
                  full(1, Q_LORA), full(1, KV_LORA), full(Q_LORA, 768), full(MLA_HEADS, LANES, KV_LORA),
                  full(MLA_HEADS, 256, LANES)],
        out_specs=(row(DQ_W), row(DQ_W), row(DQ_W), row(DV_W), row(DV_W),
                   pl.BlockSpec((MLA_HEADS, tm, MLA_PAD_W), lambda i: (0, i, 0)),
                   row(MLA_CACHE_W), row(MLA_PAD_W)),
        out_shape=outs,
        compiler_params=_cparams(("parallel",)),
        name="attn_in",
    )(h, norm_g.reshape(1, d), win, tabs, qn_g.reshape(1, -1), kvn_g.reshape(1, -1), wuq, wuk, sel)


def _lambda_value(lp, lam_init):
    a = jnp.sum(lp[0:1] * lp[1:2], axis=1, keepdims=True)
    b = jnp.sum(lp[2:3] * lp[3:4], axis=1, keepdims=True)
    return jnp.exp(a) - jnp.exp(b) + lam_init


def _online_softmax_step(s, v, m_scr, l_scr, acc_scr):
    m_prev = m_scr[...]
    m_new = jnp.maximum(m_prev, jnp.max(s, axis=1, keepdims=True))
    alpha = jnp.exp(m_prev - m_new)
    p = jnp.exp(s - m_new)
    l_scr[...] = alpha * l_scr[...] + jnp.sum(p, axis=1, keepdims=True)
    acc_scr[...] = alpha * acc_scr[...] + _mm(p.astype(BF16), v)
    m_scr[...] = m_new


def _diff_attn_kernel(q_ref, k_ref, v_ref, lp_ref, g_ref, o_ref, m_scr, l_scr, acc_scr, *, tq, lam_init):
    qi = pl.program_id(2)
    q = q_ref[...]
    lane = lax.broadcasted_iota(jnp.int32, q.shape, 1)
    zero = jnp.zeros_like(q)
    q2 = jnp.concatenate([jnp.where(lane < DIFF_D, q, zero), jnp.where(lane >= DIFF_D, q, zero)], axis=0)
    m_scr[...] = jnp.full(m_scr.shape, NEG_INF, F32)
    l_scr[...] = jnp.zeros(l_scr.shape, F32)
    acc_scr[...] = jnp.zeros(acc_scr.shape, F32)

    def block(start, masked):
        k = k_ref[pl.ds(start, tq), :]
        v = v_ref[pl.ds(start, tq), :]
        s = _nt(q2, k)
        if masked:
            r = lax.broadcasted_iota(jnp.int32, s.shape, 0) % tq
            c = lax.broadcasted_iota(jnp.int32, s.shape, 1)
            s = jnp.where(c <= r, s, NEG_INF)
        _online_softmax_step(s, v, m_scr, l_scr, acc_scr)

    def body(j, carry):
        block(pl.multiple_of(j * tq, tq), False)
        return carry

    lax.fori_loop(0, qi, body, 0)
    block(pl.multiple_of(qi * tq, tq), True)

    o = acc_scr[...] / l_scr[...]
    lam = _lambda_value(lp_ref[...], lam_init)
    od = o[:tq] - lam * o[tq:]
    od = _rms(od, g_ref[...]) * (1.0 - lam_init)
    o_ref[...] = od.astype(o_ref.dtype)


def _diff_attn_prompt(dq_b, dk_b, dv_b, lam_p, subln_g, batch, seq, lam_init, tq):
    n = dq_b.shape[0]
    nq = seq // tq
    return pl.pallas_call(
        functools.partial(_diff_attn_kernel, tq=tq, lam_init=lam_init),
        grid=(batch, DIFF_HEADS, nq),
        in_specs=[pl.BlockSpec((tq, LANES), lambda b, h, i: (b * nq + i, h)),
                  pl.BlockSpec((seq, LANES), lambda b, h, i: (b, h)),
                  pl.BlockSpec((seq, LANES), lambda b, h, i: (b, h)),
                  pl.BlockSpec((4, DIFF_D), lambda b, h, i: (0, 0)),
                  pl.BlockSpec((1, DIFF_V), lambda b, h, i: (0, 0))],
        out_specs=pl.BlockSpec((tq, LANES), lambda b, h, i: (b * nq + i, h)),
        out_shape=jax.ShapeDtypeStruct((n, DV_W), BF16),
        scratch_shapes=[pltpu.VMEM((2 * tq, 1), F32), pltpu.VMEM((2 * tq, 1), F32), pltpu.VMEM((2 * tq, DIFF_V), F32)],
        compiler_params=_cparams(("parallel", "parallel", "arbitrary")),
        name="diff_attn_prompt",
    )(dq_b, dk_b, dv_b, lam_p, subln_g.reshape(1, DIFF_V))


def _mla_attn_kernel(q_ref, k_ref, wuv_ref, o_ref, m_scr, l_scr, acc_scr, *, tq, tk):
    qi = pl.program_id(1)
    q = q_ref[...].reshape(MLA_HEADS * tq, MLA_PAD_W)
    m_scr[...] = jnp.full(m_scr.shape, NEG_INF, F32)
    l_scr[...] = jnp.zeros(l_scr.shape, F32)
    acc_scr[...] = jnp.zeros(acc_scr.shape, F32)

    def block(start, masked):
        k = k_ref[pl.ds(start, tk), :]
        s = _nt(q, k)
        if masked:
            r = qi * tq + lax.broadcasted_iota(jnp.int32, s.shape, 0) % tq
            c = start + lax.broadcasted_iota(jnp.int32, s.shape, 1)
            s = jnp.where(c <= r, s, NEG_INF)
        _online_softmax_step(s, k[:, 0:KV_LORA], m_scr, l_scr, acc_scr)

    def body(j, carry):
        block(pl.multiple_of(j * tk, tk), False)
        return carry

    n_full = (qi * tq) // tk
    lax.fori_loop(0, n_full, body, 0)
    block(pl.multiple_of(n_full * tk, tk), True)

    o = (acc_scr[...] / l_scr[...]).astype(BF16)
    out = _mm(o[0:tq], wuv_ref[0])
    for hh in range(1, MLA_HEADS):
        out = out + _mm(o[hh * tq:(hh + 1) * tq], wuv_ref[hh])
    o_ref[...] = out.astype(o_ref.dtype)


def _wuv_padded(w_uv):
    eye = jnp.eye(MLA_HEADS, dtype=w_uv.dtype)
    return jnp.einsum("hcv,hg->hcgv", w_uv, eye).reshape(MLA_HEADS, KV_LORA, MLA_HEADS * V_HEAD).astype(BF16)


def _mla_attn_prompt(mq_b, mk_b, wuv_pad, batch, seq, tq, tk):
    n = mk_b.shape[0]
    nq = seq // tq
    return pl.pallas_call(
        functools.partial(_mla_attn_kernel, tq=tq, tk=tk),
        grid=(batch, nq),
        in_specs=[pl.BlockSpec((MLA_HEADS, tq, MLA_PAD_W), lambda b, i: (0, b * nq + i, 0)),
                  pl.BlockSpec((seq, MLA_PAD_W), lambda b, i: (b, 0)),
                  pl.BlockSpec((MLA_HEADS, KV_LORA, MLA_HEADS * V_HEAD), lambda b, i: (0, 0, 0))],
        out_specs=pl.BlockSpec((tq, MLA_HEADS * V_HEAD), lambda b, i: (b * nq + i, 0)),
        out_shape=jax.ShapeDtypeStruct((n, MLA_HEADS * V_HEAD), BF16),
        scratch_shapes=[pltpu.VMEM((MLA_HEADS * tq, 1), F32), pltpu.VMEM((MLA_HEADS * tq, 1), F32),
                        pltpu.VMEM((MLA_HEADS * tq, KV_LORA), F32)],
        compiler_params=_cparams(("parallel", "arbitrary")),
        name="mla_attn_prompt",
    )(mq_b, mk_b, wuv_pad)


def _out_proj_kernel(h_ref, a_ref, b_ref, wa_ref, wb_ref, o_ref):
    o_ref[...] = h_ref[...] + _mm(a_ref[...], wa_ref[...]) + _mm(b_ref[...], wb_ref[...])


def _out_proj(h, a, b, w_out, tm):
    n, d = h.shape
    ka, kb = a.shape[1], b.shape[1]
    wa = w_out[:ka].astype(BF16)
    wb = w_out[ka:].astype(BF16)
    return pl.pallas_call(
        _out_proj_kernel,
        grid=(n // tm,),
        in_specs=[pl.BlockSpec((tm, d), lambda i: (i, 0)), pl.BlockSpec((tm, ka), lambda i: (i, 0)),
                  pl.BlockSpec((tm, kb), lambda i: (i, 0)), pl.BlockSpec((ka, d), lambda i: (0, 0)),
                  pl.BlockSpec((kb, d), lambda i: (0, 0))],
        out_specs=pl.BlockSpec((tm, d), lambda i: (i, 0)),
        out_shape=jax.ShapeDtypeStruct((n, d), F32),
        compiler_params=_cparams(("parallel",)),
        name="out_proj",
    )(h, a, b, wa, wb)


def _final_norm_kernel(h_ref, g_ref, o_ref):
    o_ref[...] = _rms(h_ref[...], g_ref[...])


def _final_norm(h, g, tm):
    n, d = h.shape
    return pl.pallas_call(
        _final_norm_kernel,
        grid=(n // tm,),
        in_specs=[pl.BlockSpec((tm, d), lambda i: (i, 0)), pl.BlockSpec((1, d), lambda i: (0, 0))],
        out_specs=pl.BlockSpec((tm, d), lambda i: (i, 0)),
        out_shape=jax.ShapeDtypeStruct((n, d), F32),
        compiler_params=_cparams(("parallel",)),
        name="final_norm",
    )(h, g.reshape(1, d))


def _sort16_desc(a):
    a = list(a)
    n = len(a)
    k = 2
    while k <= n:
        j = k // 2
        while j >= 1:
            for i in range(n):
                l = i ^ j
                if l > i:
                    hi = jnp.maximum(a[i], a[l])
                    lo = jnp.minimum(a[i], a[l])
                    a[i], a[l] = (hi, lo) if (i & k) == 0 else (lo, hi)
            j //= 2
        k *= 2
    return a


def _merge_top_desc(a, b):
    n = len(a)
    c = [jnp.maximum(a[i], b[n - 1 - i]) for i in range(n)]
    j = n // 2
    while j >= 1:
        for i in range(n):
            l = i ^ j
            if l > i:
                c[i], c[l] = jnp.maximum(c[i], c[l]), jnp.minimum(c[i], c[l])
        j //= 2
    return c


def _top16_sorted(xs):
    groups = [_sort16_desc(xs[g:g + PEER_TOPK]) for g in range(0, len(xs), PEER_TOPK)]
    while len(groups) > 1:
        groups = [_merge_top_desc(groups[i], groups[i + 1]) for i in range(0, len(groups), 2)]
    return groups[0]


def _pair_threshold(sv1, sv2):
    k = PEER_TOPK
    cur = [sv1[i] + sv2[0] for i in range(k)]
    cnt = [jnp.zeros(sv1[0].shape, jnp.int32) for _ in range(k)]
    m0 = cur[0]
    z = jnp.zeros(sv1[0].shape, F32)
    m = m0
    for step in range(k):
        m = cur[0]
        for i in range(1, k):
            m = jnp.maximum(m, cur[i])
        z = z + jnp.exp(m - m0)
        if step == k - 1:
            break
        first = jnp.full(m.shape, k, jnp.int32)
        for i in range(k - 1, -1, -1):
            first = jnp.where(cur[i] == m, i, first)
        sel = [first == i for i in range(k)]
        nsel = jnp.zeros(m.shape, jnp.int32)
        for i in range(k):
            nsel = jnp.where(sel[i], cnt[i], nsel)
        nsel = nsel + 1
        v2 = jnp.full(m.shape, NEG_INF, F32)
        for r in range(1, k):
            v2 = jnp.where(nsel == r, sv2[r], v2)
        for i in range(k):
            cnt[i] = jnp.where(sel[i], nsel, cnt[i])
            cur[i] = jnp.where(sel[i], sv1[i] + v2, cur[i])
    return m, z


def _peer_route_kernel(h_ref, g_ref, wq_ref, keys_ref, xn_ref, s1_ref, a1_ref, s2_ref, b2_ref, t16_ref, sc_scr):
    xn = _rms(h_ref[...], g_ref[...]).astype(BF16)
    xn_ref[...] = xn
    q = _mm(xn, wq_ref[...]).astype(BF16)
    nk = PEER_KEYS
    n_chunks = sc_scr.shape[1]
    for p in range(2):
        for hh in range(PEER_HEADS):
            c0 = (p * PEER_HEADS + hh) * nk
            sc = _nt(keys_ref[p, hh], q[:, c0:c0 + nk])
            for c in range(n_chunks):
                sc_scr[p, c, hh * nk:(hh + 1) * nk, :] = sc[:, c * LANES:(c + 1) * LANES]
            if p == 1:
                s2_ref[hh] = sc

    def per_chunk(c, carry):
        ls = pl.ds(pl.multiple_of(c * LANES, LANES), LANES)
        x1 = [sc_scr[0, c, pl.ds(kk, PEER_HEADS, stride=nk), :] for kk in range(nk)]
        x2 = [sc_scr[1, c, pl.ds(kk, PEER_HEADS, stride=nk), :] for kk in range(nk)]
        sv1 = _top16_sorted(x1)
        sv2 = _top16_sorted(x2)
        t16, z = _pair_threshold(sv1, sv2)
        t16_ref[:, ls] = t16
        rz = 1.0 / z
        for kk in range(nk):
            s1_ref[kk, :, ls] = x1[kk]
            a1_ref[kk, :, ls] = jnp.exp(x1[kk] - sv1[0]) * rz
        top2 = sv2[0]
        for hh in range(PEER_HEADS):
            b2_ref[hh, :, ls] = jnp.exp(sc_scr[1, c, hh * nk:(hh + 1) * nk, :] - top2[hh:hh + 1, :])
        return carry

    lax.fori_loop(0, n_chunks, per_chunk, 0)


def _peer_route(h, norm_g, wq_b, keys_b, tm):
    n, d = h.shape
    nh, nk = PEER_HEADS, PEER_KEYS
    return pl.pallas_call(
        _peer_route_kernel,
        grid=(n // tm,),
        in_specs=[pl.BlockSpec((tm, d), lambda i: (i, 0)), pl.BlockSpec((1, d), lambda i: (0, 0)),
                  pl.BlockSpec((d, 2 * nh * nk), lambda i: (0, 0)),
                  pl.BlockSpec((2, nh, nk, nk), lambda i: (0, 0, 0, 0))],
        out_specs=(pl.BlockSpec((tm, d), lambda i: (i, 0)),
                   pl.BlockSpec((nk, nh, tm), lambda i: (0, 0, i)), pl.BlockSpec((nk, nh, tm), lambda i: (0, 0, i)),
                   pl.BlockSpec((nh, nk, tm), lambda i: (0, 0, i)), pl.BlockSpec((nh, nk, tm), lambda i: (0, 0, i)),
                   pl.BlockSpec((nh, tm), lambda i: (0, i))),
        out_shape=(jax.ShapeDtypeStruct((n, d), BF16),
                   jax.ShapeDtypeStruct((nk, nh, n), F32), jax.ShapeDtypeStruct((nk, nh, n), F32),
                   jax.ShapeDtypeStruct((nh, nk, n), F32), jax.ShapeDtypeStruct((nh, nk, n), F32),
                   jax.ShapeDtypeStruct((nh, n), F32)),
        scratch_shapes=[pltpu.VMEM((2, tm // LANES, nh * nk, LANES), F32)],
        compiler_params=_cparams(("parallel",)),
        name="peer_route",
    )(h, norm_g.reshape(1, d), wq_b, keys_b)


def _gelu_tanh(x):
    return 0.5 * x * (1.0 + jnp.tanh(0.7978845608028654 * (x + 0.044715 * (x * x * x))))


def _peer_dense_kernel(xt_ref, u_ref, vt_ref, s1_ref, a1_ref, s2_ref, b2_ref, t16_ref, h_ref, o_ref,
                       acc_scr, act_scr, wt_scr, *, lc):
    ei = pl.program_id(1)
    et, tt = act_scr.shape
    nk = PEER_KEYS

    @pl.when(ei == 0)
    def _():
        acc_scr[...] = jnp.zeros(acc_scr.shape, F32)

    act_scr[...] = _mm(u_ref[...], xt_ref[...])

    def per_e1(j, carry):
        r0 = pl.multiple_of(j * nk, nk)
        for c in range(tt // lc):
            ls = slice(c * lc, (c + 1) * lc)
            gate = jnp.zeros((nk, lc), F32)
            for hh in range(PEER_HEADS):
                s1 = s1_ref[j, hh:hh + 1, ls]
                a1 = a1_ref[j, hh:hh + 1, ls]
                keep = (s2_ref[hh, :, ls] + s1) >= t16_ref[hh:hh + 1, ls]
                gate = gate + jnp.where(keep, b2_ref[hh, :, ls], 0.0) * a1
            wt_scr[pl.ds(r0, nk), ls] = (gate * _gelu_tanh(act_scr[pl.ds(r0, nk), ls])).astype(BF16)
        return carry

    lax.fori_loop(0, et // nk, per_e1, 0)
    acc_scr[...] += _mm(vt_ref[...], wt_scr[...])

    @pl.when(ei == pl.num_programs(1) - 1)
    def _():
        o_ref[...] = h_ref[...] + acc_scr[...].T


def _peer_dense(h, xt, u_b, vt_b, s1, a1, s2, b2, t16, tt, et, lc=LANES):
    n, d = h.shape
    ne = u_b.shape[0]
    nh, nk = PEER_HEADS, PEER_KEYS
    return pl.pallas_call(
        functools.partial(_peer_dense_kernel, lc=lc),
        grid=(n // tt, ne // et),
        in_specs=[pl.BlockSpec((d, tt), lambda t, e: (0, t)),
                  pl.BlockSpec((et, d), lambda t, e: (e, 0)),
                  pl.BlockSpec((d, et), lambda t, e: (0, e)),
                  pl.BlockSpec((et // nk, nh, tt), lambda t, e: (e, 0, t)),
                  pl.BlockSpec((et // nk, nh, tt), lambda t, e: (e, 0, t)),
                  pl.BlockSpec((nh, nk, tt), lambda t, e: (0, 0, t)),
                  pl.BlockSpec((nh, nk, tt), lambda t, e: (0, 0, t)),
                  pl.BlockSpec((nh, tt), lambda t, e: (0, t)),
                  pl.BlockSpec((tt, d), lambda t, e: (t, 0))],
        out_specs=pl.BlockSpec((tt, d), lambda t, e: (t, 0)),
        out_shape=jax.ShapeDtypeStruct((n, d), F32),
        scratch_shapes=[pltpu.VMEM((d, tt), F32), pltpu.VMEM((et, tt), F32), pltpu.VMEM((et, tt), BF16)],
        compiler_params=_cparams(("parallel", "arbitrary")),
        name="peer_dense",
    )(xt, u_b, vt_b, s1, a1, s2, b2, t16, h)


def _peer_weights(w_q, keys, u_tab, v_tab):
    d = w_q.shape[0]
    nh, nk = PEER_HEADS, PEER_KEYS
    wq_b = w_q.reshape(d, nh, 2, nk).transpose(0, 2, 1, 3).reshape(d, 2 * nh * nk).astype(BF16)
    keys_b = jnp.swapaxes(keys, 0, 1).astype(BF16)
    return wq_b, keys_b, u_tab.astype(BF16), v_tab.T.astype(BF16)


def _peer_layer(h, norm_g, pw, tm, tt, et):
    wq_b, keys_b, u_b, vt_b = pw
    xn_b, s1, a1, s2, b2, t16 = _peer_route(h, norm_g, wq_b, keys_b, tm)
    return _peer_dense(h, xn_b.T, u_b, vt_b, s1, a1, s2, b2, t16, tt, et)


Q_ROWS_D = 16


def _new_token_init(q, kn, vn, n_tok, m_ref, l_ref, acc_ref):
    tok = lax.broadcasted_iota(jnp.int32, (q.shape[0], 1), 0) % n_tok
    ss = []
    for j in range(n_tok):
        sj = jnp.sum(q * kn[j:j + 1, :], axis=1, keepdims=True)
        ss.append(jnp.where(tok >= j, sj, NEG_INF))
    m = ss[0]
    for sj in ss[1:]:
        m = jnp.maximum(m, sj)
    ps = [jnp.exp(sj - m) for sj in ss]
    l = ps[0]
    acc = ps[0] * vn[0:1, :]
    for j in range(1, n_tok):
        l = l + ps[j]
        acc = acc + ps[j] * vn[j:j + 1, :]
    m_ref[...] = m
    l_ref[...] = l
    acc_ref[...] = acc


def _paged_diff_kernel(pt_ref, q_ref, kn_ref, vn_ref, lp_ref, g_ref, *rest, n_pg, n_tok, lam_init):
    k_refs = rest[:n_pg]
    v_refs = rest[n_pg:2 * n_pg]
    o_ref = rest[2 * n_pg]
    m_scr, l_scr, acc_scr = rest[2 * n_pg + 1:]
    c = pl.program_id(1)

    @pl.when(c == 0)
    def _():
        kn = kn_ref[...]
        vn = vn_ref[...]
        for hh in range(DIFF_HEADS):
            sl = slice(hh * LANES, (hh + 1) * LANES)
            _new_token_init(q_ref[hh], kn[:, sl], vn[:, sl], n_tok, m_scr.at[hh], l_scr.at[hh], acc_scr.at[hh])

    for hh in range(DIFF_HEADS):
        sl = slice(hh * LANES, (hh + 1) * LANES)
        qb = q_ref[hh].astype(BF16)
        s = jnp.concatenate([_nt(qb, k_refs[j][:, sl].astype(BF16)) for j in range(n_pg)], axis=1)
        m_prev = m_scr[hh]
        m_new = jnp.maximum(m_prev, jnp.max(s, axis=1, keepdims=True))
        alpha = jnp.exp(m_prev - m_new)
        p = jnp.exp(s - m_new)
        l_scr[hh] = alpha * l_scr[hh] + jnp.sum(p, axis=1, keepdims=True)
        pb = p.astype(BF16)
        pv = _mm(pb[:, 0:LANES], v_refs[0][:, sl].astype(BF16))
        for j in range(1, n_pg):
            pv = pv + _mm(pb[:, j * LANES:(j + 1) * LANES], v_refs[j][:, sl].astype(BF16))
        acc_scr[hh] = alpha * acc_scr[hh] + pv
        m_scr[hh] = m_new

    @pl.when(c == pl.num_programs(1) - 1)
    def _():
        lam = _lambda_value(lp_ref[...], lam_init)
        for hh in range(DIFF_HEADS):
            o = acc_scr[hh] / l_scr[hh]
            od = o[0:n_tok] - lam * o[n_tok:2 * n_tok]
            od = _rms(od, g_ref[...]) * (1.0 - lam_init)
            o_ref[:, hh * LANES:(hh + 1) * LANES] = od


def _paged_diff(q, kn, vn, lam_p, subln_g, cache_k, cache_v, page_table, li, lam_init, n_pg):
    bsz, n_tok, _ = kn.shape
    n_pages = page_table.shape[1]
    page = cache_k.shape[2]
    assert page == LANES and n_pages % n_pg == 0
    ck = cache_k.reshape(cache_k.shape[0], cache_k.shape[1], page, DQ_W)
    cv = cache_v.reshape(cache_v.shape[0], cache_v.shape[1], page, DV_W)

    def page_spec(j, w):
        return pl.BlockSpec((None, None, page, w), lambda b, c, pt: (li, pt[b * n_pages + c * n_pg + j], 0, 0))

    grid_spec = pltpu.PrefetchScalarGridSpec(
        num_scalar_prefetch=1,
        grid=(bsz, n_pages // n_pg),
        in_specs=[pl.BlockSpec((None, DIFF_HEADS, Q_ROWS_D, LANES), lambda b, c, pt: (b, 0, 0, 0)),
                  pl.BlockSpec((None, n_tok, DQ_W), lambda b, c, pt: (b, 0, 0)),
                  pl.BlockSpec((None, n_tok, DV_W), lambda b, c, pt: (b, 0, 0)),
                  pl.BlockSpec((4, DIFF_D), lambda b, c, pt: (0, 0)),
                  pl.BlockSpec((1, DIFF_V), lambda b, c, pt: (0, 0))]
                 + [page_spec(j, DQ_W) for j in range(n_pg)] + [page_spec(j, DV_W) for j in range(n_pg)],
        out_specs=pl.BlockSpec((None, n_tok, DV_W), lambda b, c, pt: (b, 0, 0)),
        scratch_shapes=[pltpu.VMEM((DIFF_HEADS, Q_ROWS_D, 1), F32), pltpu.VMEM((DIFF_HEADS, Q_ROWS_D, 1), F32),
                        pltpu.VMEM((DIFF_HEADS, Q_ROWS_D, DIFF_V), F32)],
    )
    return pl.pallas_call(
        functools.partial(_paged_diff_kernel, n_pg=n_pg, n_tok=n_tok, lam_init=lam_init),
        grid_spec=grid_spec,
        out_shape=jax.ShapeDtypeStruct((bsz, n_tok, DV_W), F32),
        compiler_params=_cparams(("parallel", "arbitrary")),
        name="paged_diff",
    )(page_table.reshape(-1), q, kn, vn, lam_p, subln_g.reshape(1, DIFF_V), *([ck] * n_pg), *([cv] * n_pg))


def _paged_mla_kernel(pt_ref, q_ref, kn_ref, *rest, n_pg, n_tok):
    pg_refs = rest[:n_pg]
    o_ref = rest[n_pg]
    m_scr, l_scr, acc_scr, k_scr = rest[n_pg + 1:]
    c = pl.program_id(1)
    q = q_ref[...]

    @pl.when(c == 0)
    def _():
        k_scr[...] = jnp.zeros(k_scr.shape, BF16)
        kn = kn_ref[...]
        _new_token_init(q, kn, kn[:, 0:KV_LORA], n_tok, m_scr, l_scr, acc_scr)

    for j in range(n_pg):
        pg = pg_refs[j][...]
        k_scr[j * LANES:(j + 1) * LANES, 0:KV_LORA] = pg[:, 0:KV_LORA].astype(BF16)
        k_scr[j * LANES:(j + 1) * LANES, KV_LORA:MLA_CACHE_W] = pg[:, KV_LORA:MLA_CACHE_W].astype(BF16)
    k = k_scr[...]
    s = _nt(q.astype(BF16), k)
    _online_softmax_step(s, k[:, 0:KV_LORA], m_scr, l_scr, acc_scr)

    @pl.when(c == pl.num_programs(1) - 1)
    def _():
        o_ref[...] = acc_scr[...] / l_scr[...]


def _paged_mla(q, kn, cache_m, page_table, li, n_pg):
    bsz, n_tok, _ = kn.shape
    rows = q.shape[1]
    n_pages = page_table.shape[1]
    page = cache_m.shape[2]
    assert page == LANES and n_pages % n_pg == 0

    def page_spec(j):
        return pl.BlockSpec((None, None, page, MLA_CACHE_W), lambda b, c, pt: (li, pt[b * n_pages + c * n_pg + j], 0, 0))

    grid_spec = pltpu.PrefetchScalarGridSpec(
        num_scalar_prefetch=1,
        grid=(bsz, n_pages // n_pg),
        in_specs=[pl.BlockSpec((None, rows, MLA_PAD_W), lambda b, c, pt: (b, 0, 0)),
                  pl.BlockSpec((None, n_tok, MLA_PAD_W), lambda b, c, pt: (b, 0, 0))]
                 + [page_spec(j) for j in range(n_pg)],
        out_specs=pl.BlockSpec((None, rows, KV_LORA), lambda b, c, pt: (b, 0, 0)),
        scratch_shapes=[pltpu.VMEM((rows, 1), F32), pltpu.VMEM((rows, 1), F32), pltpu.VMEM((rows, KV_LORA), F32),
                        pltpu.VMEM((n_pg * LANES, MLA_PAD_W), BF16)],
    )
    return pl.pallas_call(
        functools.partial(_paged_mla_kernel, n_pg=n_pg, n_tok=n_tok),
        grid_spec=grid_spec,
        out_shape=jax.ShapeDtypeStruct((bsz, rows, KV_LORA), F32),
        compiler_params=_cparams(("parallel", "arbitrary")),
        name="paged_mla",
    )(page_table.reshape(-1), q, kn, *([cache_m] * n_pg))


def _head_proj_kernel(o_ref, wuv_ref, out_ref):
    out = _mm(o_ref[0], wuv_ref[0])
    for hh in range(1, MLA_HEADS):
        out = out + _mm(o_ref[hh], wuv_ref[hh])
    out_ref[...] = out.astype(out_ref.dtype)


def _head_proj(o_h, wuv_pad):
    _, n, _ = o_h.shape
    return pl.pallas_call(
        _head_proj_kernel,
        out_shape=jax.ShapeDtypeStruct((n, MLA_HEADS * V_HEAD), BF16),
        compiler_params=_cparams(()),
        name="head_proj",
    )(o_h, wuv_pad)


def _attn_sample_layer(h, bsz, n_tok, past_len, lam_init, prm, caches, li, n_pg):
    (norm_g, w_in, lam_p, subln_g, qn_g, kvn_g, w_uq, w_uk, w_uv, w_out) = prm
    cache_k, cache_v, cache_m, page_table = caches
    n = bsz * n_tok
    win, wuq, wuk, sel = _attn_in_weights(w_in, w_uq, w_uk)
    tabs = _rope_tables(past_len + jnp.arange(n) % n_tok)
    dq_b, dk, dk_b, dv, dv_b, mq_b, mkv, mk_b = _attn_in(h, tabs, 1, norm_g, win, qn_g, kvn_g, wuq, wuk, sel, n)
    q4 = dq_b.astype(F32).reshape(bsz, n_tok, DIFF_HEADS, 2, DIFF_D)
    eye = jnp.eye(2, dtype=F32)
    qd = jnp.einsum("bthmd,mk->bhmtkd", q4, eye).reshape(bsz, DIFF_HEADS, 2 * n_tok, 2 * DIFF_D)
    qd = jnp.pad(qd, ((0, 0), (0, 0), (0, Q_ROWS_D - 2 * n_tok), (0, 0)))
    od = _paged_diff(qd, dk_b.astype(F32).reshape(bsz, n_tok, DQ_W), dv_b.astype(F32).reshape(bsz, n_tok, DV_W),
                     lam_p, subln_g, cache_k, cache_v, page_table, li, lam_init, n_pg)
    qm = mq_b.astype(F32).reshape(MLA_HEADS, bsz, n_tok, MLA_PAD_W).transpose(1, 0, 2, 3).reshape(bsz, MLA_HEADS * n_tok, MLA_PAD_W)
    om = _paged_mla(qm, mk_b.astype(F32).reshape(bsz, n_tok, MLA_PAD_W), cache_m, page_table, li, n_pg)
    om = om.reshape(bsz, MLA_HEADS, n_tok, KV_LORA).transpose(1, 0, 2, 3).reshape(MLA_HEADS, n, KV_LORA).astype(BF16)
    om = _head_proj(om, _wuv_padded(w_uv))
    h = _out_proj(h, od.reshape(n, DV_W).astype(BF16), om, w_out, n)
    return h, dk, dv, mkv


def _sigmoid(x):
    return 1.0 / (1.0 + jnp.exp(-x))


def _mix_in_kernel(h_ref, g_ref, win_ref, u_ref, rkv_ref, lora_ref):
    xn = _rms(h_ref[...], g_ref[...]).astype(BF16)
    z = _mm(xn, win_ref[...])
    u_ref[...] = z[:, 0:CONV_CH] * _sigmoid(z[:, CONV_CH:2 * CONV_CH])
    rkv_ref[...] = z[:, 2 * CONV_CH:2 * CONV_CH + 3 * RWKV_W]
    lora_ref[...] = z[:, 2 * CONV_CH + 3 * RWKV_W:MIX_IN_PAD]


def _mix_in(h, norm_g, win, tm):
    n, d = h.shape
    row = lambda w: pl.BlockSpec((tm, w), lambda i: (i, 0))
    return pl.pallas_call(
        _mix_in_kernel,
        grid=(n // tm,),
        in_specs=[row(d), pl.BlockSpec((1, d), lambda i: (0, 0)), pl.BlockSpec((d, MIX_IN_PAD), lambda i: (0, 0))],
        out_specs=(row(CONV_CH), row(3 * RWKV_W), row(LORA_PAD)),
        out_shape=(jax.ShapeDtypeStruct((n, CONV_CH), F32), jax.ShapeDtypeStruct((n, 3 * RWKV_W), F32),
                   jax.ShapeDtypeStruct((n, LORA_PAD), F32)),
        compiler_params=_cparams(("parallel",)),
        name="mix_in",
    )(h, norm_g.reshape(1, d), win)


CONV_PAD_ROWS = 32


def _conv_kernel(u_ref, st_ref, cw_ref, cb_ref, lg_ref, lb_ref, c_ref, nb_ref, full_scr, *, tt):
    t_len = u_ref.shape[0]
    full_scr[0:CONV_PAD_ROWS, :] = st_ref[...]
    full_scr[CONV_PAD_ROWS:CONV_PAD_ROWS + t_len, :] = u_ref[...]
    lead = CONV_PAD_ROWS - (CONV_W - 1)

    def chunk(ci, carry):
        t0 = ci * tt if isinstance(ci, int) else pl.multiple_of(ci * tt, tt)
        acc = jnp.zeros((tt, CONV_CH), F32) + cb_ref[...]
        win = full_scr[pl.ds(t0, tt + CONV_PAD_ROWS), :]
        for w in range(CONV_W):
            acc = acc + win[w + lead:w + lead + tt, :] * cw_ref[w:w + 1, :]
        mu = jnp.mean(acc, axis=-1, keepdims=True)
        xc = acc - mu
        var = jnp.mean(xc * xc, axis=-1, keepdims=True)
        y = xc * lax.rsqrt(var + LN_EPS) * lg_ref[...] + lb_ref[...]
        c_ref[pl.ds(t0, tt), :] = (y * _sigmoid(y)).astype(c_ref.dtype)
        return carry

    if t_len == tt:
        chunk(0, 0)
    else:
        lax.fori_loop(0, t_len // tt, chunk, 0)
    nb_ref[...] = full_scr[t_len:t_len + CONV_PAD_ROWS, :]


def _conv(u3, conv_buf, cw, cb, lng, lnb, tt):
    bsz, t_len, ch = u3.shape
    st = jnp.pad(conv_buf, ((0, 0), (CONV_PAD_ROWS - (CONV_W - 1), 0), (0, 0)))
    vec = lambda: pl.BlockSpec((1, ch), lambda b: (0, 0))
    c_b, nb = pl.pallas_call(
        functools.partial(_conv_kernel, tt=tt),
        grid=(bsz,),
        in_specs=[pl.BlockSpec((None, t_len, ch), lambda b: (b, 0, 0)),
                  pl.BlockSpec((None, CONV_PAD_ROWS, ch), lambda b: (b, 0, 0)),
                  pl.BlockSpec((CONV_W, ch), lambda b: (0, 0)), vec(), vec(), vec()],
        out_specs=(pl.BlockSpec((None, t_len, ch), lambda b: (b, 0, 0)),
                   pl.BlockSpec((None, CONV_PAD_ROWS, ch), lambda b: (b, 0, 0))),
        out_shape=(jax.ShapeDtypeStruct((bsz, t_len, ch), BF16), jax.ShapeDtypeStruct((bsz, CONV_PAD_ROWS, ch), F32)),
        scratch_shapes=[pltpu.VMEM((t_len + CONV_PAD_ROWS, ch), F32)],
        compiler_params=_cparams(("parallel",)),
        name="conv",
    )(u3, st, cw, cb.reshape(1, ch), lng.reshape(1, ch), lnb.reshape(1, ch))
    return c_b, nb[:, CONV_PAD_ROWS - (CONV_W - 1):, :]


def _head_sum(x, ones_ref):
    hi = x.astype(BF16)
    lo = (x - hi.astype(F32)).astype(BF16)
    return _mm(hi, ones_ref[...]) + _mm(lo, ones_ref[...])


def _rwkv_prep_kernel(rkv_ref, prkv_ref, lora_ref, plora_ref, mu_ref, mul_ref, w0_ref, w2_ref, a0_ref, a2_ref, g2_ref,
                      kk_ref_w, ka_ref, ones_ref, r_ref, w_ref, k_ref, v_ref, kk_ref, a_ref, g_ref):
    rkv = rkv_ref[...]
    zs = rkv + (prkv_ref[...] - rkv) * mu_ref[...]
    lora = lora_ref[...]
    ls = lora + (plora_ref[...] - lora) * mul_ref[...]
    r = zs[:, 0:RWKV_W]
    k = zs[:, RWKV_W:2 * RWKV_W]
    v = zs[:, 2 * RWKV_W:3 * RWKV_W]
    wdec = _mm(jnp.tanh(ls).astype(BF16), w2_ref[...])
    x = -(w0_ref[...] + wdec)
    softplus = jnp.maximum(x, 0.0) + jnp.log(1.0 + jnp.exp(-jnp.abs(x)))
    wlog = -softplus - 0.5
    w_ref[...] = jnp.exp(-jnp.exp(wlog))
    a = _sigmoid(a0_ref[...] + _mm(ls.astype(BF16), a2_ref[...]))
    g_ref[...] = _mm(_sigmoid(ls).astype(BF16), g2_ref[...])
    kk = k * kk_ref_w[...]
    nrm = jnp.sqrt(_head_sum(kk * kk, ones_ref))
    kk_ref[...] = kk / jnp.maximum(nrm, 1e-12)
    k_ref[...] = k * (1.0 + (a - 1.0) * ka_ref[...])
    r_ref[...] = r
    v_ref[...] = v
    a_ref[...] = a


def _lora_pad(w, row0):
    out = jnp.zeros((LORA_PAD, w.shape[1]), F32)
    return lax.dynamic_update_slice(out, w.astype(F32), (row0, 0)).astype(BF16)


def _head_ones():
    idx = np.arange(RWKV_W) // RWKV_N
    return jnp.asarray((idx[:, None] == idx[None, :]).astype(np.float32), BF16)


def _rwkv_prep(rkv, prkv, lora, plora, mu, w0, w2, a0, a2, g2, k_k, k_a, tm):
    n = rkv.shape[0]
    row = lambda w: pl.BlockSpec((tm, w), lambda i: (i, 0))
    full = lambda a, b: pl.BlockSpec((a, b), lambda i: (0, 0))
    mu_rkv = mu[:3 * RWKV_W].reshape(1, -1)
    mu_l = jnp.pad(mu[3 * RWKV_W:], (0, LORA_PAD - LORA_W)).reshape(1, -1)
    o = jax.ShapeDtypeStruct((n, RWKV_W), F32)
    return pl.pallas_call(
        _rwkv_prep_kernel,
        grid=(n // tm,),
        in_specs=[row(3 * RWKV_W), row(3 * RWKV_W), row(LORA_PAD), row(LORA_PAD), full(1, 3 * RWKV_W), full(1, LORA_PAD),
                  full(1, RWKV_W), full(LORA_PAD, RWKV_W), full(1, RWKV_W), full(LORA_PAD, RWKV_W), full(LORA_PAD, RWKV_W),
                  full(1, RWKV_W), full(1, RWKV_W), full(RWKV_W, RWKV_W)],
        out_specs=(row(RWKV_W),) * 7,
        out_shape=(o,) * 7,
        compiler_params=_cparams(("parallel",)),
        name="rwkv_prep",
    )(rkv, prkv, lora, plora, mu_rkv, mu_l, w0.reshape(1, -1), _lora_pad(w2, 0), a0.reshape(1, -1), _lora_pad(a2, 32),
      _lora_pad(g2, 64), k_k.reshape(1, -1), k_a.reshape(1, -1), _head_ones())


def _wkv_scan_kernel(r_ref, w_ref, k_ref, v_ref, kk_ref, a_ref, s0_ref, o_ref, s_out_ref, s_scr):
    ti = pl.program_id(1)
    tc = r_ref.shape[0]
    nn = RWKV_N

    @pl.when(ti == 0)
    def _():
        s_scr[...] = s0_ref[...]

    def step(t, carry):
        kk = kk_ref[t]
        w = w_ref[t]
        k = k_ref[t]
        r = r_ref[t]
        v = v_ref[t]
        kka = kk * a_ref[t]
        sk = s_scr[0] * kk[0:1, :]
        for j in range(1, nn):
            sk = sk + s_scr[j] * kk[j:j + 1, :]
        y = jnp.zeros(sk.shape, F32)
        for j in range(nn):
            s_new = s_scr[j] * w[j:j + 1, :] - sk * kka[j:j + 1, :] + v * k[j:j + 1, :]
            s_scr[j] = s_new
            y = y + s_new * r[j:j + 1, :]
        o_ref[t] = y
        return carry

    lax.fori_loop(0, tc, step, 0)

    @pl.when(ti == pl.num_programs(1) - 1)
    def _():
        s_out_ref[...] = s_scr[...]


def _wkv_scan(r, w, k, v, kk, a, s0, tc):
    t_len, nn, lanes = r.shape
    seq = pl.BlockSpec((tc, nn, LANES), lambda l, t: (t, 0, l))
    st = pl.BlockSpec((nn, nn, LANES), lambda l, t: (0, 0, l))
    return pl.pallas_call(
        _wkv_scan_kernel,
        grid=(lanes // LANES, t_len // tc),
        in_specs=[seq] * 6 + [st],
        out_specs=(seq, st),
        out_shape=(jax.ShapeDtypeStruct((t_len, nn, lanes), F32), jax.ShapeDtypeStruct((nn, nn, lanes), F32)),
        scratch_shapes=[pltpu.VMEM((nn, nn, LANES), F32)],
        compiler_params=_cparams(("parallel", "arbitrary")),
        name="wkv_scan",
    )(r, w, k, v, kk, a, s0)


def _rwkv_post_kernel(o_ref, r_ref, k_ref, v_ref, g_ref, gg_ref, gb_ref, rk_ref, ones_ref, d_ref):
    o = o_ref[...]
    inv_n = 1.0 / RWKV_N
    mean = _head_sum(o, ones_ref) * inv_n
    xc = o - mean
    var = _head_sum(xc * xc, ones_ref) * inv_n
    on = xc * lax.rsqrt(var + GN_EPS) * gg_ref[...] + gb_ref[...]
    bonus = _head_sum(r_ref[...] * k_ref[...] * rk_ref[...], ones_ref) * v_ref[...]
    d_ref[...] = ((on + bonus) * g_ref[...]).astype(d_ref.dtype)


def _rwkv_post(o, r, k, v, g, gn_g, gn_b, r_k, tm):
    n = o.shape[0]
    row = pl.BlockSpec((tm, RWKV_W), lambda i: (i, 0))
    vec = pl.BlockSpec((1, RWKV_W), lambda i: (0, 0))
    return pl.pallas_call(
        _rwkv_post_kernel,
        grid=(n // tm,),
        in_specs=[row] * 5 + [vec] * 3 + [pl.BlockSpec((RWKV_W, RWKV_W), lambda i: (0, 0))],
        out_specs=row,
        out_shape=jax.ShapeDtypeStruct((n, RWKV_W), BF16),
        compiler_params=_cparams(("parallel",)),
        name="rwkv_post",
    )(o, r, k, v, g, gn_g.reshape(1, -1), gn_b.reshape(1, -1), r_k.reshape(1, -1), _head_ones())


def _mix_layer(h, bsz, t_len, prm, conv_buf, shift_prev, wkv0, tm, tt, tc):
    (norm_g, w_in, cw, cb, lng, lnb, mu, w0, w2, a0, a2, g2, k_k, k_a, r_k, gn_g, gn_b, w_out) = prm
    n, d = h.shape
    mix_in_w = 2 * CONV_CH + 3 * RWKV_W + LORA_W
    win = jnp.pad(w_in, ((0, 0), (0, MIX_IN_PAD - mix_in_w))).astype(BF16)
    u, rkv, lora = _mix_in(h, norm_g, win, tm)
    c_b, new_buf = _conv(u.reshape(bsz, t_len, CONV_CH), conv_buf, cw, cb, lng, lnb, tt)
    rkv3 = rkv.reshape(bsz, t_len, 3 * RWKV_W)
    lora3 = lora.reshape(bsz, t_len, LORA_PAD)
    sp_l = jnp.pad(shift_prev[:, 3 * RWKV_W:], ((0, 0), (0, LORA_PAD - LORA_W)))
    prkv = jnp.concatenate([shift_prev[:, None, :3 * RWKV_W], rkv3[:, :-1]], axis=1).reshape(n, -1)
    plora = jnp.concatenate([sp_l[:, None, :], lora3[:, :-1]], axis=1).reshape(n, -1)
    r, w, k, v, kk, a, g = _rwkv_prep(rkv, prkv, lora, plora, mu, w0, w2, a0, a2, g2, k_k, k_a, tm)

    def lanes_major(x):
        return x.reshape(bsz, t_len, RWKV_HEADS, RWKV_N).transpose(1, 3, 0, 2).reshape(t_len, RWKV_N, bsz * RWKV_HEADS)

    s0 = wkv0.transpose(3, 2, 0, 1).reshape(RWKV_N, RWKV_N, bsz * RWKV_HEADS)
    o_l, s_l = _wkv_scan(*(lanes_major(x) for x in (r, w, k, v, kk, a)), s0, tc)
    o = o_l.reshape(t_len, RWKV_N, bsz, RWKV_HEADS).transpose(2, 0, 3, 1).reshape(n, RWKV_W)
    d_b = _rwkv_post(o, r, k, v, g, gn_g, gn_b, r_k.reshape(-1), tm)
    h = _out_proj(h, c_b.reshape(n, CONV_CH), d_b, w_out, tm)
    new_shift = jnp.concatenate([rkv3[:, -1], lora3[:, -1, :LORA_W]], axis=-1)
    wkv = s_l.reshape(RWKV_N, RWKV_N, bsz, RWKV_HEADS).transpose(2, 3, 1, 0)
    return h, new_buf, new_shift, wkv


def _attn_prompt_layer(h, batch, seq, lam_init, prm, tm, tq_d, tq_m, tk_m):
    (norm_g, w_in, lam_p, subln_g, qn_g, kvn_g, w_uq, w_uk, w_uv, w_out) = prm
    win, wuq, wuk, sel = _attn_in_weights(w_in, w_uq, w_uk)
    tabs = _rope_tables(jnp.arange(seq))
    dq_b, dk, dk_b, dv, dv_b, mq_b, mkv, mk_b = _attn_in(h, tabs, seq // tm, norm_g, win, qn_g, kvn_g, wuq, wuk, sel, tm)
    od = _diff_attn_prompt(dq_b, dk_b, dv_b, lam_p, subln_g, batch, seq, lam_init, tq_d)
    om = _mla_attn_prompt(mq_b, mk_b, _wuv_padded(w_uv), batch, seq, tq_m, tk_m)
    h = _out_proj(h, od, om, w_out, tm)
    return h, dk, dv, mkv


TM_PROJ = 256
TQ_DIFF = 256
TQ_MLA = 128
TK_MLA = 256
TM_ROUTE = 256
TT_DENSE = 512
ET_DENSE = 1024
TT_CONV = 256
TC_SCAN = 32
PAGES_PER_STEP = 8


def kernel(x_prompt, x_sample, cache_diff_k, cache_diff_v, cache_mla, page_table, state_conv, state_shift, state_wkv, attn_norm, attn_w_in, diff_lambda, diff_subln, mla_q_norm, mla_kv_norm, mla_w_uq, mla_w_uk, mla_w_uv, attn_w_out, mix_norm, mix_w_in, conv_w, conv_b, conv_ln_g, conv_ln_b, rwkv_mu, rwkv_w0, rwkv_w2, rwkv_a0, rwkv_a2, rwkv_g2, rwkv_k_k, rwkv_k_a, rwkv_r_k, rwkv_ln_g, rwkv_ln_b, mix_w_out, ffn_norm, peer_w_q, peer_keys, peer_u, peer_v, final_norm):
    bp, sp, d = x_prompt.shape
    bs, ss, _ = x_sample.shape
    n_p, n_s = bp * sp, bs * ss
    past_len = page_table.shape[1] * cache_mla.shape[2]
    depth = ffn_norm.shape[0]
    h_p = x_prompt.reshape(n_p, d)
    h_s = x_sample.reshape(n_s, d)
    outs = {k: [] for k in ("dk_p", "dv_p", "mla_p", "conv_p", "shift_p", "wkv_p", "dk_s", "dv_s", "mla_s", "conv_s", "shift_s", "wkv_s")}
    for layer in range(depth):
        i = layer // 2
        if layer % 2 == 0:
            lam_init = 0.8 - 0.6 * math.exp(-0.3 * layer)
            prm = (attn_norm[i], attn_w_in[i], diff_lambda[i], diff_subln[i], mla_q_norm[i], mla_kv_norm[i],
                   mla_w_uq[i], mla_w_uk[i], mla_w_uv[i], attn_w_out[i])
            h_p, k_n, v_n, c_n = _attn_prompt_layer(h_p, bp, sp, lam_init, prm, TM_PROJ, TQ_DIFF, TQ_MLA, TK_MLA)
            outs["dk_p"].append(k_n.reshape(bp, sp, 2 * DIFF_HEADS, DIFF_D))
            outs["dv_p"].append(v_n.reshape(bp, sp, DIFF_HEADS, DIFF_V))
            outs["mla_p"].append(c_n.reshape(bp, sp, MLA_CACHE_W))
            h_s, k_n, v_n, c_n = _attn_sample_layer(h_s, bs, ss, past_len, lam_init, prm,
                                                    (cache_diff_k, cache_diff_v, cache_mla, page_table), i, PAGES_PER_STEP)
            outs["dk_s"].append(k_n.reshape(bs, ss, 2 * DIFF_HEADS, DIFF_D))
            outs["dv_s"].append(v_n.reshape(bs, ss, DIFF_HEADS, DIFF_V))
            outs["mla_s"].append(c_n.reshape(bs, ss, MLA_CACHE_W))
        else:
            prm = (mix_norm[i], mix_w_in[i], conv_w[i], conv_b[i], conv_ln_g[i], conv_ln_b[i], rwkv_mu[i], rwkv_w0[i],
                   rwkv_w2[i], rwkv_a0[i], rwkv_a2[i], rwkv_g2[i], rwkv_k_k[i], rwkv_k_a[i], rwkv_r_k[i], rwkv_ln_g[i],
                   rwkv_ln_b[i], mix_w_out[i])
            h_p, cb_n, sh_n, st_n = _mix_layer(
                h_p, bp, sp, prm, jnp.zeros((bp, CONV_W - 1, CONV_CH), F32), jnp.zeros((bp, 3 * RWKV_W + LORA_W), F32),
                jnp.zeros((bp, RWKV_HEADS, RWKV_N, RWKV_N), F32), TM_PROJ, TT_CONV, TC_SCAN)
            outs["conv_p"].append(cb_n)
            outs["shift_p"].append(sh_n)
            outs["wkv_p"].append(st_n)
            h_s, cb_n, sh_n, st_n = _mix_layer(h_s, bs, ss, prm, state_conv[i], state_shift[i], state_wkv[i], TM_PROJ, ss, ss)
            outs["conv_s"].append(cb_n)
            outs["shift_s"].append(sh_n)
            outs["wkv_s"].append(st_n)
        pw = _peer_weights(peer_w_q[layer], peer_keys[layer], peer_u[layer], peer_v[layer])
        h_p = _peer_layer(h_p, ffn_norm[layer], pw, TM_ROUTE, TT_DENSE, ET_DENSE)
        h_s = _peer_layer(h_s, ffn_norm[layer], pw, TM_ROUTE, TT_DENSE, ET_DENSE)
    y_p = _final_norm(h_p, final_norm, TM_PROJ).reshape(bp, sp, d)
    y_s = _final_norm(h_s, final_norm, TM_PROJ).reshape(bs, ss, d)
    st = lambda name: jnp.stack(outs[name])
    return (y_p, y_s, st("dk_p"), st("dv_p"), st("mla_p"), st("conv_p"), st("shift_p"), st("wkv_p"),
            st("dk_s"), st("dv_s"), st("mla_s"), st("conv_s"), st("shift_s"), st("wkv_s"))
```

```python
import functools
import math

import numpy as np
import jax
import jax.numpy as jnp
from jax import lax
from jax.experimental import pallas as pl
from jax.experimental.pallas import tpu as pltpu

F32 = jnp.float32
BF16 = jnp.bfloat16

NORM_EPS = 1e-6
LN_EPS = 1e-5
GN_EPS = 64e-5
ROPE_THETA = 500000.0

DIFF_HEADS = 4
DIFF_D = 64
DIFF_V = 128
DQ_W = 2 * DIFF_HEADS * DIFF_D
DV_W = DIFF_HEADS * DIFF_V
MLA_HEADS = 8
Q_LORA = 384
KV_LORA = 256
QK_NOPE = 64
QK_ROPE = 32
V_HEAD = 64
MLA_CACHE_W = KV_LORA + QK_ROPE
MLA_PAD_W = 384
ATTN_IN = 2 * DQ_W + DV_W + Q_LORA + KV_LORA + QK_ROPE
ATTN_IN_PAD = 2304
CONV_CH = 512
CONV_W = 31
RWKV_HEADS = 8
RWKV_N = 64
RWKV_W = 512
LORA_W = 160
LORA_PAD = 256
MIX_IN_PAD = 2 * CONV_CH + 3 * RWKV_W + LORA_PAD
PEER_HEADS = 8
PEER_KEYS = 128
PEER_TOPK = 16
LANES = 128
VMEM_LIMIT_MB = 56

DIFF_SCALE = DIFF_D ** -0.5
MLA_SCALE = (QK_NOPE + QK_ROPE) ** -0.5
NEG_INF = float("-inf")


def _cparams(sem, vmem_mb=VMEM_LIMIT_MB):
    return pltpu.CompilerParams(dimension_semantics=sem, vmem_limit_bytes=vmem_mb * 1024 * 1024)


def _rms(x, g, eps=NORM_EPS):
    return x * lax.rsqrt(jnp.mean(x * x, axis=-1, keepdims=True) + eps) * g


def _nt(a, b):
    return lax.dot_general(a, b, (((1,), (1,)), ((), ())), preferred_element_type=F32)


def _mm(a, b):
    return jnp.dot(a, b, preferred_element_type=F32)


def _rope_tables(pos):
    posf = pos.astype(F32)[:, None]
    inv8 = ROPE_THETA ** (-jnp.arange(8, dtype=F32) / 8)
    inv16 = ROPE_THETA ** (-jnp.arange(16, dtype=F32) / 16)
    a8 = posf * inv8[None, :]
    a16 = posf * inv16[None, :]
    c8, s8, c16, s16 = jnp.cos(a8), jnp.sin(a8), jnp.cos(a16), jnp.sin(a16)
    lane = np.arange(LANES)
    d = lane % DIFF_D
    i8 = d % 8
    i16 = lane % 16
    cd = jnp.where((d < 16)[None, :], c8[:, i8], 1.0)
    sa = jnp.where((d < 8)[None, :], -s8[:, i8], 0.0)
    sb = jnp.where(((d >= 8) & (d < 16))[None, :], s8[:, i8], 0.0)
    cm = c16[:, i16]
    sm = s16[:, i16]
    ck = jnp.where((lane < 32)[None, :], c16[:, i16], 0.0)
    ska = jnp.where((lane < 16)[None, :], -s16[:, i16], 0.0)
    skb = jnp.where(((lane >= 16) & (lane < 32))[None, :], s16[:, i16], 0.0)
    return jnp.stack([cd, sa, sb, cm, sm, ck, ska, skb]).astype(F32)


def _attn_in_kernel(h_ref, ng_ref, win_ref, tab_ref, qng_ref, kvng_ref, wuq_ref, wuk_ref, sel_ref,
                    dq_ref, dk_ref, dkb_ref, dv_ref, dvb_ref, mq_ref, mkv_ref, mkb_ref):
    xn = _rms(h_ref[...], ng_ref[...]).astype(BF16)
    z = _mm(xn, win_ref[...])
    cd, sa, sb = tab_ref[0], tab_ref[1], tab_ref[2]

    def rot(x):
        return x * cd + pltpu.roll(x, LANES - 8, 1) * sa + pltpu.roll(x, 8, 1) * sb

    for j in range(DQ_W // LANES):
        sl = slice(j * LANES, (j + 1) * LANES)
        dq_ref[:, sl] = (rot(z[:, sl]) * DIFF_SCALE).astype(BF16)
        k = rot(z[:, DQ_W + j * LANES:DQ_W + (j + 1) * LANES])
        dk_ref[:, sl] = k
        dkb_ref[:, sl] = k.astype(BF16)
    dv = z[:, 2 * DQ_W:2 * DQ_W + DV_W]
    dv_ref[...] = dv
    dvb_ref[...] = dv.astype(BF16)

    c0 = 2 * DQ_W + DV_W
    cqn = _rms(z[:, c0:c0 + Q_LORA], qng_ref[...]).astype(BF16)
    q = _mm(cqn, wuq_ref[...])
    cm, sm = tab_ref[3], tab_ref[4]
    x1 = q[:, 512:640]
    x2 = q[:, 640:768]
    r1 = (x1 * cm - x2 * sm) * MLA_SCALE
    r2 = (x2 * cm + x1 * sm) * MLA_SCALE
    rcat = jnp.concatenate([r1, r2], axis=1).astype(BF16)
    qn = q[:, :512].astype(BF16)
    for hh in range(MLA_HEADS):
        slab = qn[:, (hh // 2) * LANES:(hh // 2 + 1) * LANES]
        lat = _mm(slab, wuk_ref[hh]) * MLA_SCALE
        rt = _mm(rcat, sel_ref[hh])
        mq_ref[hh, :, 0:KV_LORA] = lat.astype(BF16)
        mq_ref[hh, :, KV_LORA:MLA_PAD_W] = rt.astype(BF16)

    c1 = c0 + Q_LORA
    ckvn = _rms(z[:, c1:c1 + KV_LORA], kvng_ref[...])
    ck, ska, skb = tab_ref[5], tab_ref[6], tab_ref[7]
    slab = z[:, c1 + KV_LORA:c1 + KV_LORA + LANES]
    kr = slab * ck + pltpu.roll(slab, LANES - 16, 1) * ska + pltpu.roll(slab, 16, 1) * skb
    mkv_ref[:, 0:KV_LORA] = ckvn
    mkv_ref[:, KV_LORA:MLA_CACHE_W] = kr[:, 0:QK_ROPE]
    mkb_ref[:, 0:KV_LORA] = ckvn.astype(BF16)
    mkb_ref[:, KV_LORA:MLA_PAD_W] = kr.astype(BF16)


def _attn_in_weights(w_in, w_uq, w_uk):
    d = w_in.shape[0]
    win = jnp.concatenate([w_in, jnp.zeros((d, ATTN_IN_PAD - ATTN_IN), w_in.dtype)], axis=1).astype(BF16)
    perm = np.zeros((MLA_HEADS * (QK_NOPE + QK_ROPE),), np.int32)
    for hh in range(MLA_HEADS):
        base = hh * (QK_NOPE + QK_ROPE)
        perm[hh * 64:(hh + 1) * 64] = base + np.arange(64)
        perm[512 + hh * 16:512 + (hh + 1) * 16] = base + 64 + np.arange(16)
        perm[640 + hh * 16:640 + (hh + 1) * 16] = base + 80 + np.arange(16)
    wuq = w_uq[:, perm].astype(BF16)
    wukt = jnp.swapaxes(w_uk, 1, 2)
    zeros = jnp.zeros_like(wukt)
    even = jnp.concatenate([wukt, zeros], axis=1)
    odd = jnp.concatenate([zeros, wukt], axis=1)
    is_even = (np.arange(MLA_HEADS) % 2 == 0)[:, None, None]
    wuk = jnp.where(is_even, even, odd).astype(BF16)
    sel = np.zeros((MLA_HEADS, 256, LANES), np.float32)
    for hh in range(MLA_HEADS):
        for i in range(16):
            sel[hh, hh * 16 + i, i] = 1.0
            sel[hh, 128 + hh * 16 + i, 16 + i] = 1.0
    return win, wuq, wuk, jnp.asarray(sel, BF16)


def _attn_in(h, tabs, n_tab_tiles, norm_g, win, qn_g, kvn_g, wuq, wuk, sel, tm):
    n, d = h.shape
    full = lambda *shape: pl.BlockSpec(shape, lambda i: (0,) * len(shape))
    row = lambda w: pl.BlockSpec((tm, w), lambda i: (i, 0))
    outs = (
        jax.ShapeDtypeStruct((n, DQ_W), BF16), jax.ShapeDtypeStruct((n, DQ_W), F32), jax.ShapeDtypeStruct((n, DQ_W), BF16),
        jax.ShapeDtypeStruct((n, DV_W), F32), jax.ShapeDtypeStruct((n, DV_W), BF16),
        jax.ShapeDtypeStruct((MLA_HEADS, n, MLA_PAD_W), BF16),
        jax.ShapeDtypeStruct((n, MLA_CACHE_W), F32), jax.ShapeDtypeStruct((n, MLA_PAD_W), BF16),
    )
    return pl.pallas_call(
        _attn_in_kernel,
        grid=(n // tm,),
        in_specs=[row(d), full(1, d), full(d, ATTN_IN_PAD),
                  pl.BlockSpec((8, tm, LANES), lambda i: (0, i % n_tab_tiles, 0)),
                  full(1, Q_LORA), full(1, KV_LORA), full(Q_LORA, 768), full(MLA_HEADS, LANES, KV_LORA),
                  full(MLA_HEADS, 256, LANES)],
        out_specs=(row(DQ_W), row(DQ_W), row(DQ_W), row(DV_W), row(DV_W),
                   pl.BlockSpec((MLA_HEADS, tm, MLA_PAD_W), lambda i: (0, i, 0)),
                   row(MLA_CACHE_W), row(MLA_PAD_W)),
        out_shape=outs,
        compiler_params=_cparams(("parallel",)),
        name="attn_in",
    )(h, norm_g.reshape(1, d), win, tabs, qn_g.reshape(1, -1), kvn_g.reshape(1, -1), wuq, wuk, sel)


def _lambda_value(lp, lam_init):
    a = jnp.sum(lp[0:1] * lp[1:2], axis=1, keepdims=True)
    b = jnp.sum(lp[2:3] * lp[3:4], axis=1, keepdims=True)
    return jnp.exp(a) - jnp.exp(b) + lam_init


def _online_softmax_step(s, v, m_scr, l_scr, acc_scr):
    m_prev = m_scr[...]
    m_new = jnp.maximum(m_prev, jnp.max(s, axis=1, keepdims=True))
    alpha = jnp.exp(m_prev - m_new)
    p = jnp.exp(s - m_new)
    l_scr[...] = alpha * l_scr[...] + jnp.sum(p, axis=1, keepdims=True)
    acc_scr[...] = alpha * acc_scr[...] + _mm(p.astype(BF16), v)
    m_scr[...] = m_new


def _diff_attn_kernel(q_ref, k_ref, v_ref, lp_ref, g_ref, o_ref, m_scr, l_scr, acc_scr, *, tq, tk, lam_init):
    qi = pl.program_id(2)
    q = q_ref[...]
    lane = lax.broadcasted_iota(jnp.int32, q.shape, 1)
    zero = jnp.zeros_like(q)
    q2 = jnp.concatenate([jnp.where(lane < DIFF_D, q, zero), jnp.where(lane >= DIFF_D, q, zero)], axis=0)
    m_scr[...] = jnp.full(m_scr.shape, NEG_INF, F32)
    l_scr[...] = jnp.zeros(l_scr.shape, F32)
    acc_scr[...] = jnp.zeros(acc_scr.shape, F32)

    def block(start, masked):
        k = k_ref[pl.ds(start, tk), :]
        v = v_ref[pl.ds(start, tk), :]
        s = _nt(q2, k)
        if masked:
            r = qi * tq + lax.broadcasted_iota(jnp.int32, s.shape, 0) % tq
            c = start + lax.broadcasted_iota(jnp.int32, s.shape, 1)
            s = jnp.where(c <= r, s, NEG_INF)
        _online_softmax_step(s, v, m_scr, l_scr, acc_scr)

    def body(j, carry):
        block(pl.multiple_of(j * tk, tk), False)
        return carry

    n_full = (qi * tq) // tk
    lax.fori_loop(0, n_full, body, 0)
    block(pl.multiple_of(n_full * tk, tk), True)

    o = acc_scr[...] / l_scr[...]
    lam = _lambda_value(lp_ref[...], lam_init)
    od = o[:tq] - lam * o[tq:]
    od = _rms(od, g_ref[...]) * (1.0 - lam_init)
    o_ref[...] = od.astype(o_ref.dtype)


def _diff_attn_prompt(dq_b, dk_b, dv_b, lam_p, subln_g, batch, seq, lam_init, tq, tk):
    n = dq_b.shape[0]
    nq = seq // tq
    return pl.pallas_call(
        functools.partial(_diff_attn_kernel, tq=tq, tk=tk, lam_init=lam_init),
        grid=(batch, DIFF_HEADS, nq),
        in_specs=[pl.BlockSpec((tq, LANES), lambda b, h, i: (b * nq + i, h)),
                  pl.BlockSpec((seq, LANES), lambda b, h, i: (b, h)),
                  pl.BlockSpec((seq, LANES), lambda b, h, i: (b, h)),
                  pl.BlockSpec((4, DIFF_D), lambda b, h, i: (0, 0)),
                  pl.BlockSpec((1, DIFF_V), lambda b, h, i: (0, 0))],
        out_specs=pl.BlockSpec((tq, LANES), lambda b, h, i: (b * nq + i, h)),
        out_shape=jax.ShapeDtypeStruct((n, DV_W), BF16),
        scratch_shapes=[pltpu.VMEM((2 * tq, 1), F32), pltpu.VMEM((2 * tq, 1), F32), pltpu.VMEM((2 * tq, DIFF_V), F32)],
        compiler_params=_cparams(("parallel", "parallel", "arbitrary")),
        name="diff_attn_prompt",
    )(dq_b, dk_b, dv_b, lam_p, subln_g.reshape(1, DIFF_V))


def _mla_attn_kernel(q_ref, k_ref, wuv_ref, o_ref, m_scr, l_scr, acc_scr, *, tq, tk):
    qi = pl.program_id(1)
    q = q_ref[...].reshape(MLA_HEADS * tq, MLA_PAD_W)
    m_scr[...] = jnp.full(m_scr.shape, NEG_INF, F32)
    l_scr[...] = jnp.zeros(l_scr.shape, F32)
    acc_scr[...] = jnp.zeros(acc_scr.shape, F32)

    def block(start, masked):
        k = k_ref[pl.ds(start, tk), :]
        s = _nt(q, k)
        if masked:
            r = qi * tq + lax.broadcasted_iota(jnp.int32, s.shape, 0) % tq
            c = start + lax.broadcasted_iota(jnp.int32, s.shape, 1)
            s = jnp.where(c <= r, s, NEG_INF)
        _online_softmax_step(s, k[:, 0:KV_LORA], m_scr, l_scr, acc_scr)

    def body(j, carry):
        block(pl.multiple_of(j * tk, tk), False)
        return carry

    n_full = (qi * tq) // tk
    lax.fori_loop(0, n_full, body, 0)
    block(pl.multiple_of(n_full * tk, tk), True)

    o = (acc_scr[...] / l_scr[...]).astype(BF16)
    out = _mm(o[0:tq], wuv_ref[0])
    for hh in range(1, MLA_HEADS):
        out = out + _mm(o[hh * tq:(hh + 1) * tq], wuv_ref[hh])
    o_ref[...] = out.astype(o_ref.dtype)


def _wuv_padded(w_uv):
    eye = jnp.eye(MLA_HEADS, dtype=w_uv.dtype)
    return jnp.einsum("hcv,hg->hcgv", w_uv, eye).reshape(MLA_HEADS, KV_LORA, MLA_HEADS * V_HEAD).astype(BF16)


def _mla_attn_prompt(mq_b, mk_b, wuv_pad, batch, seq, tq, tk):
    n = mk_b.shape[0]
    nq = seq // tq
    return pl.pallas_call(
        functools.partial(_mla_attn_kernel, tq=tq, tk=tk),
        grid=(batch, nq),
        in_specs=[pl.BlockSpec((MLA_HEADS, tq, MLA_PAD_W), lambda b, i: (0, b * nq + i, 0)),
                  pl.BlockSpec((seq, MLA_PAD_W), lambda b, i: (b, 0)),
                  pl.BlockSpec((MLA_HEADS, KV_LORA, MLA_HEADS * V_HEAD), lambda b, i: (0, 0, 0))],
        out_specs=pl.BlockSpec((tq, MLA_HEADS * V_HEAD), lambda b, i: (b * nq + i, 0)),
        out_shape=jax.ShapeDtypeStruct((n, MLA_HEADS * V_HEAD), BF16),
        scratch_shapes=[pltpu.VMEM((MLA_HEADS * tq, 1), F32), pltpu.VMEM((MLA_HEADS * tq, 1), F32),
                        pltpu.VMEM((MLA_HEADS * tq, KV_LORA), F32)],
        compiler_params=_cparams(("parallel", "arbitrary")),
        name="mla_attn_prompt",
    )(mq_b, mk_b, wuv_pad)


def _out_proj_kernel(h_ref, a_ref, b_ref, wa_ref, wb_ref, o_ref):
    o_ref[...] = h_ref[...] + _mm(a_ref[...], wa_ref[...]) + _mm(b_ref[...], wb_ref[...])


def _out_proj(h, a, b, w_out, tm):
    n, d = h.shape
    ka, kb = a.shape[1], b.shape[1]
    wa = w_out[:ka].astype(BF16)
    wb = w_out[ka:].astype(BF16)
    return pl.pallas_call(
        _out_proj_kernel,
        grid=(n // tm,),
        in_specs=[pl.BlockSpec((tm, d), lambda i: (i, 0)), pl.BlockSpec((tm, ka), lambda i: (i, 0)),
                  pl.BlockSpec((tm, kb), lambda i: (i, 0)), pl.BlockSpec((ka, d), lambda i: (0, 0)),
                  pl.BlockSpec((kb, d), lambda i: (0, 0))],
        out_specs=pl.BlockSpec((tm, d), lambda i: (i, 0)),
        out_shape=jax.ShapeDtypeStruct((n, d), F32),
        compiler_params=_cparams(("parallel",)),
        name="out_proj",
    )(h, a, b, wa, wb)


def _final_norm_kernel(h_ref, g_ref, o_ref):
    o_ref[...] = _rms(h_ref[...], g_ref[...])


def _final_norm(h, g, tm):
    n, d = h.shape
    return pl.pallas_call(
        _final_norm_kernel,
        grid=(n // tm,),
        in_specs=[pl.BlockSpec((tm, d), lambda i: (i, 0)), pl.BlockSpec((1, d), lambda i: (0, 0))],
        out_specs=pl.BlockSpec((tm, d), lambda i: (i, 0)),
        out_shape=jax.ShapeDtypeStruct((n, d), F32),
        compiler_params=_cparams(("parallel",)),
        name="final_norm",
    )(h, g.reshape(1, d))


def _sort16_desc(a):
    a = list(a)
    n = len(a)
    k = 2
    while k <= n:
        j = k // 2
        while j >= 1:
            for i in range(n):
                l = i ^ j
                if l > i:
                    hi = jnp.maximum(a[i], a[l])
                    lo = jnp.minimum(a[i], a[l])
                    a[i], a[l] = (hi, lo) if (i & k) == 0 else (lo, hi)
            j //= 2
        k *= 2
    return a


def _merge_top_desc(a, b):
    n = len(a)
    c = [jnp.maximum(a[i], b[n - 1 - i]) for i in range(n)]
    j = n // 2
    while j >= 1:
        for i in range(n):
            l = i ^ j
            if l > i:
                c[i], c[l] = jnp.maximum(c[i], c[l]), jnp.minimum(c[i], c[l])
        j //= 2
    return c


def _top16_sorted(xs):
    groups = [_sort16_desc(xs[g:g + PEER_TOPK]) for g in range(0, len(xs), PEER_TOPK)]
    while len(groups) > 1:
        groups = [_merge_top_desc(groups[i], groups[i + 1]) for i in range(0, len(groups), 2)]
    return groups[0]


def _pair_threshold(sv1, sv2):
    k = PEER_TOPK
    cur = [sv1[i] + sv2[0] for i in range(k)]
    cnt = [jnp.zeros(sv1[0].shape, jnp.int32) for _ in range(k)]
    m0 = cur[0]
    z = jnp.zeros(sv1[0].shape, F32)
    m = m0
    for step in range(k):
        m = cur[0]
        for i in range(1, k):
            m = jnp.maximum(m, cur[i])
        z = z + jnp.exp(m - m0)
        if step == k - 1:
            break
        first = jnp.full(m.shape, k, jnp.int32)
        for i in range(k - 1, -1, -1):
            first = jnp.where(cur[i] == m, i, first)
        sel = [first == i for i in range(k)]
        nsel = jnp.zeros(m.shape, jnp.int32)
        for i in range(k):
            nsel = jnp.where(sel[i], cnt[i], nsel)
        nsel = nsel + 1
        v2 = jnp.full(m.shape, NEG_INF, F32)
        for r in range(1, k):
            v2 = jnp.where(nsel == r, sv2[r], v2)
        for i in range(k):
            cnt[i] = jnp.where(sel[i], nsel, cnt[i])
            cur[i] = jnp.where(sel[i], sv1[i] + v2, cur[i])
    return m, z


def _peer_route_kernel(h_ref, g_ref, wq_ref, keys_ref, xn_ref, q1_ref, a1_ref, r2_ref, b2_ref, sc_scr):
    xn = _rms(h_ref[...], g_ref[...]).astype(BF16)
    xn_ref[...] = xn
    q = _mm(xn, wq_ref[...]).astype(BF16)
    nk = PEER_KEYS
    n_chunks = sc_scr.shape[1]
    for p in range(2):
        for hh in range(PEER_HEADS):
            c0 = (p * PEER_HEADS + hh) * nk
            sc = _nt(keys_ref[p, hh], q[:, c0:c0 + nk])
            for c in range(n_chunks):
                sc_scr[p, c, hh * nk:(hh + 1) * nk, :] = sc[:, c * LANES:(c + 1) * LANES]

    def per_chunk(c, carry):
        ls = pl.ds(pl.multiple_of(c * LANES, LANES), LANES)
        x1 = [sc_scr[0, c, pl.ds(kk, PEER_HEADS, stride=nk), :] for kk in range(nk)]
        x2 = [sc_scr[1, c, pl.ds(kk, PEER_HEADS, stride=nk), :] for kk in range(nk)]
        sv1 = _top16_sorted(x1)
        sv2 = _top16_sorted(x2)
        t16, z = _pair_threshold(sv1, sv2)
        rz = 1.0 / z
        for kk in range(nk):
            cnt = jnp.zeros(t16.shape, F32)
            for k2 in range(PEER_TOPK):
                cnt = cnt + jnp.where(x1[kk] + sv2[k2] >= t16, 1.0, 0.0)
            q1_ref[kk, :, ls] = cnt
            a1_ref[kk, :, ls] = jnp.exp(x1[kk] - sv1[0]) * rz
        for hh in range(PEER_HEADS):
            s2 = sc_scr[1, c, hh * nk:(hh + 1) * nk, :]
            rank = jnp.zeros(s2.shape, F32)
            for k2 in range(PEER_TOPK):
                rank = rank + jnp.where(sv2[k2][hh:hh + 1, :] > s2, 1.0, 0.0)
            r2_ref[hh, :, ls] = rank.astype(r2_ref.dtype)
            b2_ref[hh, :, ls] = jnp.exp(s2 - sv2[0][hh:hh + 1, :]).astype(b2_ref.dtype)
        return carry

    lax.fori_loop(0, n_chunks, per_chunk, 0)


def _peer_route(h, norm_g, wq_b, keys_b, tm):
    n, d = h.shape
    nh, nk = PEER_HEADS, PEER_KEYS
    return pl.pallas_call(
        _peer_route_kernel,
        grid=(n // tm,),
        in_specs=[pl.BlockSpec((tm, d), lambda i: (i, 0)), pl.BlockSpec((1, d), lambda i: (0, 0)),
                  pl.BlockSpec((d, 2 * nh * nk), lambda i: (0, 0)),
                  pl.BlockSpec((2, nh, nk, nk), lambda i: (0, 0, 0, 0))],
        out_specs=(pl.BlockSpec((tm, d), lambda i: (i, 0)),
                   pl.BlockSpec((nk, nh, tm), lambda i: (0, 0, i)), pl.BlockSpec((nk, nh, tm), lambda i: (0, 0, i)),
                   pl.BlockSpec((nh, nk, tm), lambda i: (0, 0, i)), pl.BlockSpec((nh, nk, tm), lambda i: (0, 0, i))),
        out_shape=(jax.ShapeDtypeStruct((n, d), BF16),
                   jax.ShapeDtypeStruct((nk, nh, n), F32), jax.ShapeDtypeStruct((nk, nh, n), F32),
                   jax.ShapeDtypeStruct((nh, nk, n), BF16), jax.ShapeDtypeStruct((nh, nk, n), BF16)),
        scratch_shapes=[pltpu.VMEM((2, tm // LANES, nh * nk, LANES), F32)],
        compiler_params=_cparams(("parallel",)),
        name="peer_route",
    )(h, norm_g.reshape(1, d), wq_b, keys_b)


def _gelu_tanh(x):
    return 0.5 * x * (1.0 + jnp.tanh(0.7978845608028654 * (x + 0.044715 * (x * x * x))))


def _rows_bf16(ref, j, hh, ls, rows):
    r = jnp.broadcast_to(ref[j, hh:hh + 1, ls], (16, ls.stop - ls.start)).astype(BF16)
    return jnp.concatenate([r] * (rows // 16), axis=0)


def _peer_dense_kernel(xt_ref, u_ref, vt_ref, q1_ref, a1_ref, r2_ref, b2_ref, h_ref, o_ref, acc_scr, wt_scr, *, lc):
    ei = pl.program_id(1)
    et, tt = wt_scr.shape
    nk = PEER_KEYS

    @pl.when(ei == 0)
    def _():
        acc_scr[...] = jnp.zeros(acc_scr.shape, F32)

    act = _mm(u_ref[...], xt_ref[...]).astype(BF16)
    for j in range(et // nk):
        rs = slice(j * nk, (j + 1) * nk)
        for c in range(tt // lc):
            ls = slice(c * lc, (c + 1) * lc)
            gate = jnp.zeros((nk, lc), BF16)
            for hh in range(PEER_HEADS):
                keep = r2_ref[hh, :, ls] < _rows_bf16(q1_ref, j, hh, ls, nk)
                gate = gate + jnp.where(keep, b2_ref[hh, :, ls], jnp.zeros((), BF16)) * _rows_bf16(a1_ref, j, hh, ls, nk)
            wt_scr[rs, ls] = gate * _gelu_tanh(act[rs, ls])
    acc_scr[...] += _mm(vt_ref[...], wt_scr[...])

    @pl.when(ei == pl.num_programs(1) - 1)
    def _():
        o_ref[...] = h_ref[...] + acc_scr[...].T


def _peer_dense(h, xt, u_b, vt_b, q1, a1, r2, b2, tt, et, lc=2 * LANES):
    n, d = h.shape
    ne = u_b.shape[0]
    nh, nk = PEER_HEADS, PEER_KEYS
    return pl.pallas_call(
        functools.partial(_peer_dense_kernel, lc=lc),
        grid=(n // tt, ne // et),
        in_specs=[pl.BlockSpec((d, tt), lambda t, e: (0, t)),
                  pl.BlockSpec((et, d), lambda t, e: (e, 0)),
                  pl.BlockSpec((d, et), lambda t, e: (0, e)),
                  pl.BlockSpec((et // nk, nh, tt), lambda t, e: (e, 0, t)),
                  pl.BlockSpec((et // nk, nh, tt), lambda t, e: (e, 0, t)),
                  pl.BlockSpec((nh, nk, tt), lambda t, e: (0, 0, t)),
                  pl.BlockSpec((nh, nk, tt), lambda t, e: (0, 0, t)),
                  pl.BlockSpec((tt, d), lambda t, e: (t, 0))],
        out_specs=pl.BlockSpec((tt, d), lambda t, e: (t, 0)),
        out_shape=jax.ShapeDtypeStruct((n, d), F32),
        scratch_shapes=[pltpu.VMEM((d, tt), F32), pltpu.VMEM((et, tt), BF16)],
        compiler_params=_cparams(("parallel", "arbitrary")),
        name="peer_dense",
    )(xt, u_b, vt_b, q1, a1, r2, b2, h)


def _peer_weights(w_q, keys, u_tab, v_tab):
    d = w_q.shape[0]
    nh, nk = PEER_HEADS, PEER_KEYS
    wq_b = w_q.reshape(d, nh, 2, nk).transpose(0, 2, 1, 3).reshape(d, 2 * nh * nk).astype(BF16)
    keys_b = jnp.swapaxes(keys, 0, 1).astype(BF16)
    return wq_b, keys_b, u_tab.astype(BF16), v_tab.T.astype(BF16)


def _peer_layer(h, norm_g, pw, tm, tt, et):
    wq_b, keys_b, u_b, vt_b = pw
    xn_b, q1, a1, r2, b2 = _peer_route(h, norm_g, wq_b, keys_b, tm)
    return _peer_dense(h, xn_b.T, u_b, vt_b, q1, a1, r2, b2, tt, et)


Q_ROWS_D = 16


def _new_token_init(q, kn, vn, n_tok, m_ref, l_ref, acc_ref):
    tok = lax.broadcasted_iota(jnp.int32, (q.shape[0], 1), 0) % n_tok
    ss = []
    for j in range(n_tok):
        sj = jnp.sum(q * kn[j:j + 1, :], axis=1, keepdims=True)
        ss.append(jnp.where(tok >= j, sj, NEG_INF))
    m = ss[0]
    for sj in ss[1:]:
        m = jnp.maximum(m, sj)
    ps = [jnp.exp(sj - m) for sj in ss]
    l = ps[0]
    acc = ps[0] * vn[0:1, :]
    for j in range(1, n_tok):
        l = l + ps[j]
        acc = acc + ps[j] * vn[j:j + 1, :]
    m_ref[...] = m
    l_ref[...] = l
    acc_ref[...] = acc


def _paged_diff_kernel(pt_ref, q_ref, kn_ref, vn_ref, lp_ref, g_ref, *rest, n_pg, n_tok, lam_init):
    k_refs = rest[:n_pg]
    v_refs = rest[n_pg:2 * n_pg]
    o_ref = rest[2 * n_pg]
    m_scr, l_scr, acc_scr = rest[2 * n_pg + 1:]
    c = pl.program_id(1)

    @pl.when(c == 0)
    def _():
        kn = kn_ref[...]
        vn = vn_ref[...]
        for hh in range(DIFF_HEADS):
            sl = slice(hh * LANES, (hh + 1) * LANES)
            _new_token_init(q_ref[hh], kn[:, sl], vn[:, sl], n_tok, m_scr.at[hh], l_scr.at[hh], acc_scr.at[hh])

    for hh in range(DIFF_HEADS):
        sl = slice(hh * LANES, (hh + 1) * LANES)
        qb = q_ref[hh].astype(BF16)
        s = jnp.concatenate([_mm(qb, k_refs[j][sl, :].astype(BF16)) for j in range(n_pg)], axis=1)
        m_prev = m_scr[hh]
        m_new = jnp.maximum(m_prev, jnp.max(s, axis=1, keepdims=True))
        alpha = jnp.exp(m_prev - m_new)
        p = jnp.exp(s - m_new)
        l_scr[hh] = alpha * l_scr[hh] + jnp.sum(p, axis=1, keepdims=True)
        pb = p.astype(BF16)
        pv = _mm(pb[:, 0:LANES], v_refs[0][:, hh, :].astype(BF16))
        for j in range(1, n_pg):
            pv = pv + _mm(pb[:, j * LANES:(j + 1) * LANES], v_refs[j][:, hh, :].astype(BF16))
        acc_scr[hh] = alpha * acc_scr[hh] + pv
        m_scr[hh] = m_new

    @pl.when(c == pl.num_programs(1) - 1)
    def _():
        lam = _lambda_value(lp_ref[...], lam_init)
        for hh in range(DIFF_HEADS):
            o = acc_scr[hh] / l_scr[hh]
            od = o[0:n_tok] - lam * o[n_tok:2 * n_tok]
            od = _rms(od, g_ref[...]) * (1.0 - lam_init)
            o_ref[:, hh * LANES:(hh + 1) * LANES] = od


def _paged_diff(q, kn, vn, lam_p, subln_g, cache_k, cache_v, page_table, li, lam_init, n_pg):
    bsz, n_tok, _ = kn.shape
    n_pages = page_table.shape[1]
    page = cache_k.shape[2]
    assert page == LANES and n_pages % n_pg == 0
    ck = jnp.transpose(cache_k, (0, 1, 3, 4, 2)).reshape(cache_k.shape[0], cache_k.shape[1], DQ_W, page)
    cv = cache_v

    def k_spec(j):
        return pl.BlockSpec((None, None, DQ_W, page), lambda b, c, pt: (li, pt[b * n_pages + c * n_pg + j], 0, 0))

    def v_spec(j):
        return pl.BlockSpec((None, None, page, DIFF_HEADS, DIFF_V),
                            lambda b, c, pt: (li, pt[b * n_pages + c * n_pg + j], 0, 0, 0))

    grid_spec = pltpu.PrefetchScalarGridSpec(
        num_scalar_prefetch=1,
        grid=(bsz, n_pages // n_pg),
        in_specs=[pl.BlockSpec((None, DIFF_HEADS, Q_ROWS_D, LANES), lambda b, c, pt: (b, 0, 0, 0)),
                  pl.BlockSpec((None, n_tok, DQ_W), lambda b, c, pt: (b, 0, 0)),
                  pl.BlockSpec((None, n_tok, DV_W), lambda b, c, pt: (b, 0, 0)),
                  pl.BlockSpec((4, DIFF_D), lambda b, c, pt: (0, 0)),
                  pl.BlockSpec((1, DIFF_V), lambda b, c, pt: (0, 0))]
                 + [k_spec(j) for j in range(n_pg)] + [v_spec(j) for j in range(n_pg)],
        out_specs=pl.BlockSpec((None, n_tok, DV_W), lambda b, c, pt: (b, 0, 0)),
        scratch_shapes=[pltpu.VMEM((DIFF_HEADS, Q_ROWS_D, 1), F32), pltpu.VMEM((DIFF_HEADS, Q_ROWS_D, 1), F32),
                        pltpu.VMEM((DIFF_HEADS, Q_ROWS_D, DIFF_V), F32)],
    )
    return pl.pallas_call(
        functools.partial(_paged_diff_kernel, n_pg=n_pg, n_tok=n_tok, lam_init=lam_init),
        grid_spec=grid_spec,
        out_shape=jax.ShapeDtypeStruct((bsz, n_tok, DV_W), F32),
        compiler_params=_cparams(("parallel", "arbitrary")),
        name="paged_diff",
    )(page_table.reshape(-1), q, kn, vn, lam_p, subln_g.reshape(1, DIFF_V), *([ck] * n_pg), *([cv] * n_pg))


def _paged_mla_kernel(pt_ref, q_ref, kn_ref, *rest, n_pg, n_tok):
    pg_refs = rest[:n_pg]
    o_ref = rest[n_pg]
    m_scr, l_scr, acc_scr = rest[n_pg + 1:]
    c = pl.program_id(1)
    q = q_ref[...]

    @pl.when(c == 0)
    def _():
        kn = kn_ref[...]
        _new_token_init(q, kn, kn[:, 0:KV_LORA], n_tok, m_scr, l_scr, acc_scr)

    qb = q.astype(BF16)
    q_lat = qb[:, 0:KV_LORA]
    q_rot = qb[:, KV_LORA:KV_LORA + LANES]
    kts = [pg_refs[j][...].astype(BF16) for j in range(n_pg)]
    s = jnp.concatenate([_mm(q_lat, kt[0:KV_LORA]) + _mm(q_rot[:, 0:QK_ROPE], kt[KV_LORA:MLA_CACHE_W]) for kt in kts], axis=1)
    m_prev = m_scr[...]
    m_new = jnp.maximum(m_prev, jnp.max(s, axis=1, keepdims=True))
    alpha = jnp.exp(m_prev - m_new)
    p = jnp.exp(s - m_new)
    l_scr[...] = alpha * l_scr[...] + jnp.sum(p, axis=1, keepdims=True)
    pb = p.astype(BF16)
    pv = _nt(pb[:, 0:LANES], kts[0][0:KV_LORA])
    for j in range(1, n_pg):
        pv = pv + _nt(pb[:, j * LANES:(j + 1) * LANES], kts[j][0:KV_LORA])
    acc_scr[...] = alpha * acc_scr[...] + pv
    m_scr[...] = m_new

    @pl.when(c == pl.num_programs(1) - 1)
    def _():
        o_ref[...] = acc_scr[...] / l_scr[...]


def _paged_mla(q, kn, cache_m, page_table, li, n_pg):
    bsz, n_tok, _ = kn.shape
    rows = q.shape[1]
    n_pages = page_table.shape[1]
    page = cache_m.shape[2]
    assert page == LANES and n_pages % n_pg == 0

    cache_t = jnp.swapaxes(cache_m, 2, 3)

    def page_spec(j):
        return pl.BlockSpec((None, None, MLA_CACHE_W, page), lambda b, c, pt: (li, pt[b * n_pages + c * n_pg + j], 0, 0))

    grid_spec = pltpu.PrefetchScalarGridSpec(
        num_scalar_prefetch=1,
        grid=(bsz, n_pages // n_pg),
        in_specs=[pl.BlockSpec((None, rows, MLA_PAD_W), lambda b, c, pt: (b, 0, 0)),
                  pl.BlockSpec((None, n_tok, MLA_PAD_W), lambda b, c, pt: (b, 0, 0))]
                 + [page_spec(j) for j in range(n_pg)],
        out_specs=pl.BlockSpec((None, rows, KV_LORA), lambda b, c, pt: (b, 0, 0)),
        scratch_shapes=[pltpu.VMEM((rows, 1), F32), pltpu.VMEM((rows, 1), F32), pltpu.VMEM((rows, KV_LORA), F32)],
    )
    return pl.pallas_call(
        functools.partial(_paged_mla_kernel, n_pg=n_pg, n_tok=n_tok),
        grid_spec=grid_spec,
        out_shape=jax.ShapeDtypeStruct((bsz, rows, KV_LORA), F32),
        compiler_params=_cparams(("parallel", "arbitrary")),
        name="paged_mla",
    )(page_table.reshape(-1), q, kn, *([cache_t] * n_pg))


def _head_proj_kernel(o_ref, wuv_ref, out_ref):
    out = _mm(o_ref[0], wuv_ref[0])
    for hh in range(1, MLA_HEADS):
        out = out + _mm(o_ref[hh], wuv_ref[hh])
    out_ref[...] = out.astype(out_ref.dtype)


def _head_proj(o_h, wuv_pad):
    _, n, _ = o_h.shape
    return pl.pallas_call(
        _head_proj_kernel,
        out_shape=jax.ShapeDtypeStruct((n, MLA_HEADS * V_HEAD), BF16),
        compiler_params=_cparams(()),
        name="head_proj",
    )(o_h, wuv_pad)


def _attn_sample_layer(h, bsz, n_tok, past_len, lam_init, prm, caches, li, n_pg):
    (norm_g, w_in, lam_p, subln_g, qn_g, kvn_g, w_uq, w_uk, w_uv, w_out) = prm
    cache_k, cache_v, cache_m, page_table = caches
    n = bsz * n_tok
    win, wuq, wuk, sel = _attn_in_weights(w_in, w_uq, w_uk)
    tabs = _rope_tables(past_len + jnp.arange(n) % n_tok)
    dq_b, dk, dk_b, dv, dv_b, mq_b, mkv, mk_b = _attn_in(h, tabs, 1, norm_g, win, qn_g, kvn_g, wuq, wuk, sel, n)
    q4 = dq_b.astype(F32).reshape(bsz, n_tok, DIFF_HEADS, 2, DIFF_D)
    eye = jnp.eye(2, dtype=F32)
    qd = jnp.einsum("bthmd,mk->bhmtkd", q4, eye).reshape(bsz, DIFF_HEADS, 2 * n_tok, 2 * DIFF_D)
    qd = jnp.pad(qd, ((0, 0), (0, 0), (0, Q_ROWS_D - 2 * n_tok), (0, 0)))
    od = _paged_diff(qd, dk_b.astype(F32).reshape(bsz, n_tok, DQ_W), dv_b.astype(F32).reshape(bsz, n_tok, DV_W),
                     lam_p, subln_g, cache_k, cache_v, page_table, li, lam_init, n_pg)
    qm = mq_b.astype(F32).reshape(MLA_HEADS, bsz, n_tok, MLA_PAD_W).transpose(1, 0, 2, 3).reshape(bsz, MLA_HEADS * n_tok, MLA_PAD_W)
    om = _paged_mla(qm, mk_b.astype(F32).reshape(bsz, n_tok, MLA_PAD_W), cache_m, page_table, li, n_pg)
    om = om.reshape(bsz, MLA_HEADS, n_tok, KV_LORA).transpose(1, 0, 2, 3).reshape(MLA_HEADS, n, KV_LORA).astype(BF16)
    om = _head_proj(om, _wuv_padded(w_uv))
    h = _out_proj(h, od.reshape(n, DV_W).astype(BF16), om, w_out, n)
    return h, dk, dv, mkv


def _sigmoid(x):
    return 1.0 / (1.0 + jnp.exp(-x))


def _mix_in_kernel(h_ref, g_ref, win_ref, u_ref, rkv_ref, lora_ref):
    xn = _rms(h_ref[...], g_ref[...]).astype(BF16)
    z = _mm(xn, win_ref[...])
    u_ref[...] = z[:, 0:CONV_CH] * _sigmoid(z[:, CONV_CH:2 * CONV_CH])
    rkv_ref[...] = z[:, 2 * CONV_CH:2 * CONV_CH + 3 * RWKV_W]
    lora_ref[...] = z[:, 2 * CONV_CH + 3 * RWKV_W:MIX_IN_PAD]


def _mix_in(h, norm_g, win, tm):
    n, d = h.shape
    row = lambda w: pl.BlockSpec((tm, w), lambda i: (i, 0))
    return pl.pallas_call(
        _mix_in_kernel,
        grid=(n // tm,),
        in_specs=[row(d), pl.BlockSpec((1, d), lambda i: (0, 0)), pl.BlockSpec((d, MIX_IN_PAD), lambda i: (0, 0))],
        out_specs=(row(CONV_CH), row(3 * RWKV_W), row(LORA_PAD)),
        out_shape=(jax.ShapeDtypeStruct((n, CONV_CH), F32), jax.ShapeDtypeStruct((n, 3 * RWKV_W), F32),
                   jax.ShapeDtypeStruct((n, LORA_PAD), F32)),
        compiler_params=_cparams(("parallel",)),
        name="mix_in",
    )(h, norm_g.reshape(1, d), win)


CONV_PAD_ROWS = 32


def _conv_kernel(u_ref, st_ref, cw_ref, cb_ref, lg_ref, lb_ref, c_ref, nb_ref, full_scr, *, tt):
    t_len = u_ref.shape[0]
    full_scr[0:CONV_PAD_ROWS, :] = st_ref[...]
    full_scr[CONV_PAD_ROWS:CONV_PAD_ROWS + t_len, :] = u_ref[...]
    lead = CONV_PAD_ROWS - (CONV_W - 1)

    def chunk(ci, carry):
        t0 = ci * tt if isinstance(ci, int) else pl.multiple_of(ci * tt, tt)
        acc = jnp.zeros((tt, CONV_CH), F32) + cb_ref[...]
        win = full_scr[pl.ds(t0, tt + CONV_PAD_ROWS), :]
        for w in range(CONV_W):
            acc = acc + win[w + lead:w + lead + tt, :] * cw_ref[w:w + 1, :]
        mu = jnp.mean(acc, axis=-1, keepdims=True)
        xc = acc - mu
        var = jnp.mean(xc * xc, axis=-1, keepdims=True)
        y = xc * lax.rsqrt(var + LN_EPS) * lg_ref[...] + lb_ref[...]
        c_ref[pl.ds(t0, tt), :] = (y * _sigmoid(y)).astype(c_ref.dtype)
        return carry

    if t_len == tt:
        chunk(0, 0)
    else:
        lax.fori_loop(0, t_len // tt, chunk, 0)
    nb_ref[...] = full_scr[t_len:t_len + CONV_PAD_ROWS, :]


def _conv(u3, conv_buf, cw, cb, lng, lnb, tt):
    bsz, t_len, ch = u3.shape
    st = jnp.pad(conv_buf, ((0, 0), (CONV_PAD_ROWS - (CONV_W - 1), 0), (0, 0)))
    vec = lambda: pl.BlockSpec((1, ch), lambda b: (0, 0))
    c_b, nb = pl.pallas_call(
        functools.partial(_conv_kernel, tt=tt),
        grid=(bsz,),
        in_specs=[pl.BlockSpec((None, t_len, ch), lambda b: (b, 0, 0)),
                  pl.BlockSpec((None, CONV_PAD_ROWS, ch), lambda b: (b, 0, 0)),
                  pl.BlockSpec((CONV_W, ch), lambda b: (0, 0)), vec(), vec(), vec()],
        out_specs=(pl.BlockSpec((None, t_len, ch), lambda b: (b, 0, 0)),
                   pl.BlockSpec((None, CONV_PAD_ROWS, ch), lambda b: (b, 0, 0))),
        out_shape=(jax.ShapeDtypeStruct((bsz, t_len, ch), BF16), jax.ShapeDtypeStruct((bsz, CONV_PAD_ROWS, ch), F32)),
        scratch_shapes=[pltpu.VMEM((t_len + CONV_PAD_ROWS, ch), F32)],
        compiler_params=_cparams(("parallel",)),
        name="conv",
    )(u3, st, cw, cb.reshape(1, ch), lng.reshape(1, ch), lnb.reshape(1, ch))
    return c_b, nb[:, CONV_PAD_ROWS - (CONV_W - 1):, :]


def _head_sum(x, ones_ref):
    hi = x.astype(BF16)
    lo = (x - hi.astype(F32)).astype(BF16)
    return _mm(hi, ones_ref[...]) + _mm(lo, ones_ref[...])


def _rwkv_prep_kernel(rkv_ref, prkv_ref, lora_ref, plora_ref, mu_ref, mul_ref, w0_ref, w2_ref, a0_ref, a2_ref, g2_ref,
                      kk_ref_w, ka_ref, ones_ref, r_ref, w_ref, k_ref, v_ref, kk_ref, a_ref, g_ref):
    rkv = rkv_ref[...]
    zs = rkv + (prkv_ref[...] - rkv) * mu_ref[...]
    lora = lora_ref[...]
    ls = lora + (plora_ref[...] - lora) * mul_ref[...]
    r = zs[:, 0:RWKV_W]
    k = zs[:, RWKV_W:2 * RWKV_W]
    v = zs[:, 2 * RWKV_W:3 * RWKV_W]
    wdec = _mm(jnp.tanh(ls).astype(BF16), w2_ref[...])
    x = -(w0_ref[...] + wdec)
    softplus = jnp.maximum(x, 0.0) + jnp.log(1.0 + jnp.exp(-jnp.abs(x)))
    wlog = -softplus - 0.5
    w_ref[...] = jnp.exp(-jnp.exp(wlog))
    a = _sigmoid(a0_ref[...] + _mm(ls.astype(BF16), a2_ref[...]))
    g_ref[...] = _mm(_sigmoid(ls).astype(BF16), g2_ref[...])
    kk = k * kk_ref_w[...]
    nrm = jnp.sqrt(_head_sum(kk * kk, ones_ref))
    kk_ref[...] = kk / jnp.maximum(nrm, 1e-12)
    k_ref[...] = k * (1.0 + (a - 1.0) * ka_ref[...])
    r_ref[...] = r
    v_ref[...] = v
    a_ref[...] = a


def _lora_pad(w, row0):
    out = jnp.zeros((LORA_PAD, w.shape[1]), F32)
    return lax.dynamic_update_slice(out, w.astype(F32), (row0, 0)).astype(BF16)


def _head_ones():
    idx = np.arange(RWKV_W) // RWKV_N
    return jnp.asarray((idx[:, None] == idx[None, :]).astype(np.float32), BF16)


def _rwkv_prep(rkv, prkv, lora, plora, mu, w0, w2, a0, a2, g2, k_k, k_a, tm):
    n = rkv.shape[0]
    row = lambda w: pl.BlockSpec((tm, w), lambda i: (i, 0))
    full = lambda a, b: pl.BlockSpec((a, b), lambda i: (0, 0))
    mu_rkv = mu[:3 * RWKV_W].reshape(1, -1)
    mu_l = jnp.pad(mu[3 * RWKV_W:], (0, LORA_PAD - LORA_W)).reshape(1, -1)
    o = jax.ShapeDtypeStruct((n, RWKV_W), F32)
    return pl.pallas_call(
        _rwkv_prep_kernel,
        grid=(n // tm,),
        in_specs=[row(3 * RWKV_W), row(3 * RWKV_W), row(LORA_PAD), row(LORA_PAD), full(1, 3 * RWKV_W), full(1, LORA_PAD),
                  full(1, RWKV_W), full(LORA_PAD, RWKV_W), full(1, RWKV_W), full(LORA_PAD, RWKV_W), full(LORA_PAD, RWKV_W),
                  full(1, RWKV_W), full(1, RWKV_W), full(RWKV_W, RWKV_W)],
        out_specs=(row(RWKV_W),) * 7,
        out_shape=(o,) * 7,
        compiler_params=_cparams(("parallel",)),
        name="rwkv_prep",
    )(rkv, prkv, lora, plora, mu_rkv, mu_l, w0.reshape(1, -1), _lora_pad(w2, 0), a0.reshape(1, -1), _lora_pad(a2, 32),
      _lora_pad(g2, 64), k_k.reshape(1, -1), k_a.reshape(1, -1), _head_ones())


def _wkv_scan_kernel(r_ref, w_ref, k_ref, v_ref, kk_ref, a_ref, s0_ref, o_ref, s_out_ref, s_scr):
    ti = pl.program_id(1)
    tc = r_ref.shape[0]
    nn = RWKV_N

    @pl.when(ti == 0)
    def _():
        s_scr[...] = s0_ref[...]

    def step(t, carry):
        kk = kk_ref[t]
        w = w_ref[t]
        k = k_ref[t]
        r = r_ref[t]
        v = v_ref[t]
        kka = kk * a_ref[t]
        sk = s_scr[0] * kk[0:1, :]
        for j in range(1, nn):
            sk = sk + s_scr[j] * kk[j:j + 1, :]
        y = jnp.zeros(sk.shape, F32)
        for j in range(nn):
            s_new = s_scr[j] * w[j:j + 1, :] - sk * kka[j:j + 1, :] + v * k[j:j + 1, :]
            s_scr[j] = s_new
            y = y + s_new * r[j:j + 1, :]
        o_ref[t] = y
        return carry

    lax.fori_loop(0, tc, step, 0)

    @pl.when(ti == pl.num_programs(1) - 1)
    def _():
        s_out_ref[...] = s_scr[...]


def _wkv_scan(r, w, k, v, kk, a, s0, tc):
    t_len, nn, lanes = r.shape
    seq = pl.BlockSpec((tc, nn, LANES), lambda l, t: (t, 0, l))
    st = pl.BlockSpec((nn, nn, LANES), lambda l, t: (0, 0, l))
    return pl.pallas_call(
        _wkv_scan_kernel,
        grid=(lanes // LANES, t_len // tc),
        in_specs=[seq] * 6 + [st],
        out_specs=(seq, st),
        out_shape=(jax.ShapeDtypeStruct((t_len, nn, lanes), F32), jax.ShapeDtypeStruct((nn, nn, lanes), F32)),
        scratch_shapes=[pltpu.VMEM((nn, nn, LANES), F32)],
        compiler_params=_cparams(("parallel", "arbitrary")),
        name="wkv_scan",
    )(r, w, k, v, kk, a, s0)


def _rwkv_post_kernel(o_ref, r_ref, k_ref, v_ref, g_ref, gg_ref, gb_ref, rk_ref, ones_ref, d_ref):
    o = o_ref[...]
    inv_n = 1.0 / RWKV_N
    mean = _head_sum(o, ones_ref) * inv_n
    xc = o - mean
    var = _head_sum(xc * xc, ones_ref) * inv_n
    on = xc * lax.rsqrt(var + GN_EPS) * gg_ref[...] + gb_ref[...]
    bonus = _head_sum(r_ref[...] * k_ref[...] * rk_ref[...], ones_ref) * v_ref[...]
    d_ref[...] = ((on + bonus) * g_ref[...]).astype(d_ref.dtype)


def _rwkv_post(o, r, k, v, g, gn_g, gn_b, r_k, tm):
    n = o.shape[0]
    row = pl.BlockSpec((tm, RWKV_W), lambda i: (i, 0))
    vec = pl.BlockSpec((1, RWKV_W), lambda i: (0, 0))
    return pl.pallas_call(
        _rwkv_post_kernel,
        grid=(n // tm,),
        in_specs=[row] * 5 + [vec] * 3 + [pl.BlockSpec((RWKV_W, RWKV_W), lambda i: (0, 0))],
        out_specs=row,
        out_shape=jax.ShapeDtypeStruct((n, RWKV_W), BF16),
        compiler_params=_cparams(("parallel",)),
        name="rwkv_post",
    )(o, r, k, v, g, gn_g.reshape(1, -1), gn_b.reshape(1, -1), r_k.reshape(1, -1), _head_ones())


def _mix_layer(h, bsz, t_len, prm, conv_buf, shift_prev, wkv0, tm, tt, tc):
    (norm_g, w_in, cw, cb, lng, lnb, mu, w0, w2, a0, a2, g2, k_k, k_a, r_k, gn_g, gn_b, w_out) = prm
    n, d = h.shape
    mix_in_w = 2 * CONV_CH + 3 * RWKV_W + LORA_W
    win = jnp.pad(w_in, ((0, 0), (0, MIX_IN_PAD - mix_in_w))).astype(BF16)
    u, rkv, lora = _mix_in(h, norm_g, win, tm)
    c_b, new_buf = _conv(u.reshape(bsz, t_len, CONV_CH), conv_buf, cw, cb, lng, lnb, tt)
    rkv3 = rkv.reshape(bsz, t_len, 3 * RWKV_W)
    lora3 = lora.reshape(bsz, t_len, LORA_PAD)
    sp_l = jnp.pad(shift_prev[:, 3 * RWKV_W:], ((0, 0), (0, LORA_PAD - LORA_W)))
    prkv = jnp.concatenate([shift_prev[:, None, :3 * RWKV_W], rkv3[:, :-1]], axis=1).reshape(n, -1)
    plora = jnp.concatenate([sp_l[:, None, :], lora3[:, :-1]], axis=1).reshape(n, -1)
    r, w, k, v, kk, a, g = _rwkv_prep(rkv, prkv, lora, plora, mu, w0, w2, a0, a2, g2, k_k, k_a, tm)

    def lanes_major(x):
        return x.reshape(bsz, t_len, RWKV_HEADS, RWKV_N).transpose(1, 3, 0, 2).reshape(t_len, RWKV_N, bsz * RWKV_HEADS)

    s0 = wkv0.transpose(3, 2, 0, 1).reshape(RWKV_N, RWKV_N, bsz * RWKV_HEADS)
    o_l, s_l = _wkv_scan(*(lanes_major(x) for x in (r, w, k, v, kk, a)), s0, tc)
    o = o_l.reshape(t_len, RWKV_N, bsz, RWKV_HEADS).transpose(2, 0, 3, 1).reshape(n, RWKV_W)
    d_b = _rwkv_post(o, r, k, v, g, gn_g, gn_b, r_k.reshape(-1), tm)
    h = _out_proj(h, c_b.reshape(n, CONV_CH), d_b, w_out, tm)
    new_shift = jnp.concatenate([rkv3[:, -1], lora3[:, -1, :LORA_W]], axis=-1)
    wkv = s_l.reshape(RWKV_N, RWKV_N, bsz, RWKV_HEADS).transpose(2, 3, 1, 0)
    return h, new_buf, new_shift, wkv


def _attn_prompt_layer(h, batch, seq, lam_init, prm, tm, tq_d, tk_d, tq_m, tk_m):
    (norm_g, w_in, lam_p, subln_g, qn_g, kvn_g, w_uq, w_uk, w_uv, w_out) = prm
    win, wuq, wuk, sel = _attn_in_weights(w_in, w_uq, w_uk)
    tabs = _rope_tables(jnp.arange(seq))
    dq_b, dk, dk_b, dv, dv_b, mq_b, mkv, mk_b = _attn_in(h, tabs, seq // tm, norm_g, win, qn_g, kvn_g, wuq, wuk, sel, tm)
    od = _diff_attn_prompt(dq_b, dk_b, dv_b, lam_p, subln_g, batch, seq, lam_init, tq_d, tk_d)
    om = _mla_attn_prompt(mq_b, mk_b, _wuv_padded(w_uv), batch, seq, tq_m, tk_m)
    h = _out_proj(h, od, om, w_out, tm)
    return h, dk, dv, mkv


TM_PROJ = 256
TQ_DIFF = 256
TK_DIFF = 2048
TQ_MLA = 128
TK_MLA = 1024
TM_ROUTE = 256
TT_DENSE = 512
ET_DENSE = 2048
TT_CONV = 256
TC_SCAN = 32
PAGES_PER_STEP = 8


def kernel(x_prompt, x_sample, cache_diff_k, cache_diff_v, cache_mla, page_table, state_conv, state_shift, state_wkv, attn_norm, attn_w_in, diff_lambda, diff_subln, mla_q_norm, mla_kv_norm, mla_w_uq, mla_w_uk, mla_w_uv, attn_w_out, mix_norm, mix_w_in, conv_w, conv_b, conv_ln_g, conv_ln_b, rwkv_mu, rwkv_w0, rwkv_w2, rwkv_a0, rwkv_a2, rwkv_g2, rwkv_k_k, rwkv_k_a, rwkv_r_k, rwkv_ln_g, rwkv_ln_b, mix_w_out, ffn_norm, peer_w_q, peer_keys, peer_u, peer_v, final_norm):
    bp, sp, d = x_prompt.shape
    bs, ss, _ = x_sample.shape
    n_p, n_s = bp * sp, bs * ss
    past_len = page_table.shape[1] * cache_mla.shape[2]
    depth = ffn_norm.shape[0]
    h_p = x_prompt.reshape(n_p, d)
    h_s = x_sample.reshape(n_s, d)
    outs = {k: [] for k in ("dk_p", "dv_p", "mla_p", "conv_p", "shift_p", "wkv_p", "dk_s", "dv_s", "mla_s", "conv_s", "shift_s", "wkv_s")}
    for layer in range(depth):
        i = layer // 2
        if layer % 2 == 0:
            lam_init = 0.8 - 0.6 * math.exp(-0.3 * layer)
            prm = (attn_norm[i], attn_w_in[i], diff_lambda[i], diff_subln[i], mla_q_norm[i], mla_kv_norm[i],
                   mla_w_uq[i], mla_w_uk[i], mla_w_uv[i], attn_w_out[i])
            h_p, k_n, v_n, c_n = _attn_prompt_layer(h_p, bp, sp, lam_init, prm, TM_PROJ, TQ_DIFF, min(TK_DIFF, sp), TQ_MLA, min(TK_MLA, sp))
            outs["dk_p"].append(k_n.reshape(bp, sp, 2 * DIFF_HEADS, DIFF_D))
            outs["dv_p"].append(v_n.reshape(bp, sp, DIFF_HEADS, DIFF_V))
            outs["mla_p"].append(c_n.reshape(bp, sp, MLA_CACHE_W))
            h_s, k_n, v_n, c_n = _attn_sample_layer(h_s, bs, ss, past_len, lam_init, prm,
                                                    (cache_diff_k, cache_diff_v, cache_mla, page_table), i, PAGES_PER_STEP)
            outs["dk_s"].append(k_n.reshape(bs, ss, 2 * DIFF_HEADS, DIFF_D))
            outs["dv_s"].append(v_n.reshape(bs, ss, DIFF_HEADS, DIFF_V))
            outs["mla_s"].append(c_n.reshape(bs, ss, MLA_CACHE_W))
        else:
            prm = (mix_norm[i], mix_w_in[i], conv_w[i], conv_b[i], conv_ln_g[i], conv_ln_b[i], rwkv_mu[i], rwkv_w0[i],
                   rwkv_w2[i], rwkv_a0[i], rwkv_a2[i], rwkv_g2[i], rwkv_k_k[i], rwkv_k_a[i], rwkv_r_k[i], rwkv_ln_g[i],
                   rwkv_ln_b[i], mix_w_out[i])
            h_p, cb_n, sh_n, st_n = _mix_layer(
                h_p, bp, sp, prm, jnp.zeros((bp, CONV_W - 1, CONV_CH), F32), jnp.zeros((bp, 3 * RWKV_W + LORA_W), F32),
                jnp.zeros((bp, RWKV_HEADS, RWKV_N, RWKV_N), F32), TM_PROJ, TT_CONV, TC_SCAN)
            outs["conv_p"].append(cb_n)
            outs["shift_p"].append(sh_n)
            outs["wkv_p"].append(st_n)
            h_s, cb_n, sh_n, st_n = _mix_layer(h_s, bs, ss, prm, state_conv[i], state_shift[i], state_wkv[i], TM_PROJ, ss, ss)
            outs["conv_s"].append(cb_n)
            outs["shift_s"].append(sh_n)
            outs["wkv_s"].append(st_n)
        pw = _peer_weights(peer_w_q[layer], peer_keys[layer], peer_u[layer], peer_v[layer])
        h_p = _peer_layer(h_p, ffn_norm[layer], pw, TM_ROUTE, TT_DENSE, ET_DENSE)
        h_s = _peer_layer(h_s, ffn_norm[layer], pw, TM_ROUTE, TT_DENSE, ET_DENSE)
    y_p = _final_norm(h_p, final_norm, TM_PROJ).reshape(bp, sp, d)
    y_s = _final_norm(h_s, final_norm, TM_PROJ).reshape(bs, ss, d)
    st = lambda name: jnp.stack(outs[name])
    return (y_p, y_s, st("dk_p"), st("dv_p"), st("mla_p"), st("conv_p"), st("shift_p"), st("wkv_p"),
            st("dk_s"), st("dv_s"), st("mla_s"), st("conv_s"), st("shift_s"), st("wkv_s"))
```

```python
import functools
import math

import numpy as np
import jax
import jax.numpy as jnp
from jax import lax
from jax.experimental import pallas as pl
from jax.experimental.pallas import tpu as pltpu

F32 = jnp.float32
BF16 = jnp.bfloat16

NORM_EPS = 1e-6
LN_EPS = 1e-5
GN_EPS = 64e-5
ROPE_THETA = 500000.0

DIFF_HEADS = 4
DIFF_D = 64
DIFF_V = 128
DQ_W = 2 * DIFF_HEADS * DIFF_D
DV_W = DIFF_HEADS * DIFF_V
MLA_HEADS = 8
Q_LORA = 384
KV_LORA = 256
QK_NOPE = 64
QK_ROPE = 32
V_HEAD = 64
MLA_CACHE_W = KV_LORA + QK_ROPE
MLA_PAD_W = 384
ATTN_IN = 2 * DQ_W + DV_W + Q_LORA + KV_LORA + QK_ROPE
ATTN_IN_PAD = 2304
CONV_CH = 512
CONV_W = 31
RWKV_HEADS = 8
RWKV_N = 64
RWKV_W = 512
LORA_W = 160
LORA_PAD = 256
MIX_IN_PAD = 2 * CONV_CH + 3 * RWKV_W + LORA_PAD
PEER_HEADS = 8
PEER_KEYS = 128
PEER_TOPK = 16
LANES = 128
VMEM_LIMIT_MB = 56

DIFF_SCALE = DIFF_D ** -0.5
MLA_SCALE = (QK_NOPE + QK_ROPE) ** -0.5
NEG_INF = float("-inf")


def _cparams(sem, vmem_mb=VMEM_LIMIT_MB):
    return pltpu.CompilerParams(dimension_semantics=sem, vmem_limit_bytes=vmem_mb * 1024 * 1024)


def _rms(x, g, eps=NORM_EPS):
    return x * lax.rsqrt(jnp.mean(x * x, axis=-1, keepdims=True) + eps) * g


def _nt(a, b):
    return lax.dot_general(a, b, (((1,), (1,)), ((), ())), preferred_element_type=F32)


def _mm(a, b):
    return jnp.dot(a, b, preferred_element_type=F32)


def _rope_tables(pos):
    posf = pos.astype(F32)[:, None]
    inv8 = ROPE_THETA ** (-jnp.arange(8, dtype=F32) / 8)
    inv16 = ROPE_THETA ** (-jnp.arange(16, dtype=F32) / 16)
    a8 = posf * inv8[None, :]
    a16 = posf * inv16[None, :]
    c8, s8, c16, s16 = jnp.cos(a8), jnp.sin(a8), jnp.cos(a16), jnp.sin(a16)
    lane = np.arange(LANES)
    d = lane % DIFF_D
    i8 = d % 8
    i16 = lane % 16
    cd = jnp.where((d < 16)[None, :], c8[:, i8], 1.0)
    sa = jnp.where((d < 8)[None, :], -s8[:, i8], 0.0)
    sb = jnp.where(((d >= 8) & (d < 16))[None, :], s8[:, i8], 0.0)
    cm = c16[:, i16]
    sm = s16[:, i16]
    ck = jnp.where((lane < 32)[None, :], c16[:, i16], 0.0)
    ska = jnp.where((lane < 16)[None, :], -s16[:, i16], 0.0)
    skb = jnp.where(((lane >= 16) & (lane < 32))[None, :], s16[:, i16], 0.0)
    return jnp.stack([cd, sa, sb, cm, sm, ck, ska, skb]).astype(F32)


def _attn_in_kernel(h_ref, ng_ref, win_ref, tab_ref, qng_ref, kvng_ref, wuq_ref, wuk_ref, sel_ref,
                    dq_ref, dk_ref, dkb_ref, dv_ref, dvb_ref, mq_ref, mkv_ref, mkb_ref):
    xn = _rms(h_ref[...], ng_ref[...]).astype(BF16)
    z = _mm(xn, win_ref[...])
    cd, sa, sb = tab_ref[0], tab_ref[1], tab_ref[2]

    def rot(x):
        return x * cd + pltpu.roll(x, LANES - 8, 1) * sa + pltpu.roll(x, 8, 1) * sb

    for j in range(DQ_W // LANES):
        sl = slice(j * LANES, (j + 1) * LANES)
        dq_ref[:, sl] = (rot(z[:, sl]) * DIFF_SCALE).astype(BF16)
        k = rot(z[:, DQ_W + j * LANES:DQ_W + (j + 1) * LANES])
        dk_ref[:, sl] = k
        dkb_ref[:, sl] = k.astype(BF16)
    dv = z[:, 2 * DQ_W:2 * DQ_W + DV_W]
    dv_ref[...] = dv
    dvb_ref[...] = dv.astype(BF16)

    c0 = 2 * DQ_W + DV_W
    cqn = _rms(z[:, c0:c0 + Q_LORA], qng_ref[...]).astype(BF16)
    q = _mm(cqn, wuq_ref[...])
    cm, sm = tab_ref[3], tab_ref[4]
    x1 = q[:, 512:640]
    x2 = q[:, 640:768]
    r1 = (x1 * cm - x2 * sm) * MLA_SCALE
    r2 = (x2 * cm + x1 * sm) * MLA_SCALE
    rcat = jnp.concatenate([r1, r2], axis=1).astype(BF16)
    qn = q[:, :512].astype(BF16)
    for hh in range(MLA_HEADS):
        slab = qn[:, (hh // 2) * LANES:(hh // 2 + 1) * LANES]
        lat = _mm(slab, wuk_ref[hh]) * MLA_SCALE
        rt = _mm(rcat, sel_ref[hh])
        mq_ref[hh, :, 0:KV_LORA] = lat.astype(BF16)
        mq_ref[hh, :, KV_LORA:MLA_PAD_W] = rt.astype(BF16)

    c1 = c0 + Q_LORA
    ckvn = _rms(z[:, c1:c1 + KV_LORA], kvng_ref[...])
    ck, ska, skb = tab_ref[5], tab_ref[6], tab_ref[7]
    slab = z[:, c1 + KV_LORA:c1 + KV_LORA + LANES]
    kr = slab * ck + pltpu.roll(slab, LANES - 16, 1) * ska + pltpu.roll(slab, 16, 1) * skb
    mkv_ref[:, 0:KV_LORA] = ckvn
    mkv_ref[:, KV_LORA:MLA_CACHE_W] = kr[:, 0:QK_ROPE]
    mkb_ref[:, 0:KV_LORA] = ckvn.astype(BF16)
    mkb_ref[:, KV_LORA:MLA_PAD_W] = kr.astype(BF16)


def _attn_in_weights(w_in, w_uq, w_uk):
    d = w_in.shape[0]
    win = jnp.concatenate([w_in, jnp.zeros((d, ATTN_IN_PAD - ATTN_IN), w_in.dtype)], axis=1).astype(BF16)
    perm = np.zeros((MLA_HEADS * (QK_NOPE + QK_ROPE),), np.int32)
    for hh in range(MLA_HEADS):
        base = hh * (QK_NOPE + QK_ROPE)
        perm[hh * 64:(hh + 1) * 64] = base + np.arange(64)
        perm[512 + hh * 16:512 + (hh + 1) * 16] = base + 64 + np.arange(16)
        perm[640 + hh * 16:640 + (hh + 1) * 16] = base + 80 + np.arange(16)
    wuq = w_uq[:, perm].astype(BF16)
    wukt = jnp.swapaxes(w_uk, 1, 2)
    zeros = jnp.zeros_like(wukt)
    even = jnp.concatenate([wukt, zeros], axis=1)
    odd = jnp.concatenate([zeros, wukt], axis=1)
    is_even = (np.arange(MLA_HEADS) % 2 == 0)[:, None, None]
    wuk = jnp.where(is_even, even, odd).astype(BF16)
    sel = np.zeros((MLA_HEADS, 256, LANES), np.float32)
    for hh in range(MLA_HEADS):
        for i in range(16):
            sel[hh, hh * 16 + i, i] = 1.0
            sel[hh, 128 + hh * 16 + i, 16 + i] = 1.0
    return win, wuq, wuk, jnp.asarray(sel, BF16)


def _attn_in(h, tabs, n_tab_tiles, norm_g, win, qn_g, kvn_g, wuq, wuk, sel, tm):
    n, d = h.shape
    full = lambda *shape: pl.BlockSpec(shape, lambda i: (0,) * len(shape))
    row = lambda w: pl.BlockSpec((tm, w), lambda i: (i, 0))
    outs = (
        jax.ShapeDtypeStruct((n, DQ_W), BF16), jax.ShapeDtypeStruct((n, DQ_W), F32), jax.ShapeDtypeStruct((n, DQ_W), BF16),
        jax.ShapeDtypeStruct((n, DV_W), F32), jax.ShapeDtypeStruct((n, DV_W), BF16),
        jax.ShapeDtypeStruct((MLA_HEADS, n, MLA_PAD_W), BF16),
        jax.ShapeDtypeStruct((n, MLA_CACHE_W), F32), jax.ShapeDtypeStruct((n, MLA_PAD_W), BF16),
    )
    return pl.pallas_call(
        _attn_in_kernel,
        grid=(n // tm,),
        in_specs=[row(d), full(1, d), full(d, ATTN_IN_PAD),
                  pl.BlockSpec((8, tm, LANES), lambda i: (0, i % n_tab_tiles, 0)),
                  full(1, Q_LORA), full(1, KV_LORA), full(Q_LORA, 768), full(MLA_HEADS, LANES, KV_LORA),
                  full(MLA_HEADS, 256, LANES)],
        out_specs=(row(DQ_W), row(DQ_W), row(DQ_W), row(DV_W), row(DV_W),
                   pl.BlockSpec((MLA_HEADS, tm, MLA_PAD_W), lambda i: (0, i, 0)),
                   row(MLA_CACHE_W), row(MLA_PAD_W)),
        out_shape=outs,
        compiler_params=_cparams(("parallel",)),
        name="attn_in",
    )(h, norm_g.reshape(1, d), win, tabs, qn_g.reshape(1, -1), kvn_g.reshape(1, -1), wuq, wuk, sel)


def _lambda_value(lp, lam_init):
    a = jnp.sum(lp[0:1] * lp[1:2], axis=1, keepdims=True)
    b = jnp.sum(lp[2:3] * lp[3:4], axis=1, keepdims=True)
    return jnp.exp(a) - jnp.exp(b) + lam_init


def _online_softmax_step(s, v, m_scr, l_scr, acc_scr):
    m_prev = m_scr[...]
    m_new = jnp.maximum(m_prev, jnp.max(s, axis=1, keepdims=True))
    alpha = jnp.exp(m_prev - m_new)
    p = jnp.exp(s - m_new)
    l_scr[...] = alpha * l_scr[...] + jnp.sum(p, axis=1, keepdims=True)
    acc_scr[...] = alpha * acc_scr[...] + _mm(p.astype(BF16), v)
    m_scr[...] = m_new


def _diff_attn_kernel(q_ref, k_ref, v_ref, lp_ref, g_ref, o_ref, m_scr, l_scr, acc_scr, *, tq, tk, lam_init):
    qi = pl.program_id(2)
    q = q_ref[...]
    lane = lax.broadcasted_iota(jnp.int32, q.shape, 1)
    zero = jnp.zeros_like(q)
    q2 = jnp.concatenate([jnp.where(lane < DIFF_D, q, zero), jnp.where(lane >= DIFF_D, q, zero)], axis=0)
    m_scr[...] = jnp.full(m_scr.shape, NEG_INF, F32)
    l_scr[...] = jnp.zeros(l_scr.shape, F32)
    acc_scr[...] = jnp.zeros(acc_scr.shape, F32)

    def block(start, masked):
        k = k_ref[pl.ds(start, tk), :]
        v = v_ref[pl.ds(start, tk), :]
        s = _nt(q2, k)
        if masked:
            r = qi * tq + lax.broadcasted_iota(jnp.int32, s.shape, 0) % tq
            c = start + lax.broadcasted_iota(jnp.int32, s.shape, 1)
            s = jnp.where(c <= r, s, NEG_INF)
        _online_softmax_step(s, v, m_scr, l_scr, acc_scr)

    def body(j, carry):
        block(pl.multiple_of(j * tk, tk), False)
        return carry

    n_full = (qi * tq) // tk
    lax.fori_loop(0, n_full, body, 0)
    block(pl.multiple_of(n_full * tk, tk), True)

    o = acc_scr[...] / l_scr[...]
    lam = _lambda_value(lp_ref[...], lam_init)
    od = o[:tq] - lam * o[tq:]
    od = _rms(od, g_ref[...]) * (1.0 - lam_init)
    o_ref[...] = od.astype(o_ref.dtype)


def _diff_attn_prompt(dq_b, dk_b, dv_b, lam_p, subln_g, batch, seq, lam_init, tq, tk):
    n = dq_b.shape[0]
    nq = seq // tq
    return pl.pallas_call(
        functools.partial(_diff_attn_kernel, tq=tq, tk=tk, lam_init=lam_init),
        grid=(batch, DIFF_HEADS, nq),
        in_specs=[pl.BlockSpec((tq, LANES), lambda b, h, i: (b * nq + i, h)),
                  pl.BlockSpec((seq, LANES), lambda b, h, i: (b, h)),
                  pl.BlockSpec((seq, LANES), lambda b, h, i: (b, h)),
                  pl.BlockSpec((4, DIFF_D), lambda b, h, i: (0, 0)),
                  pl.BlockSpec((1, DIFF_V), lambda b, h, i: (0, 0))],
        out_specs=pl.BlockSpec((tq, LANES), lambda b, h, i: (b * nq + i, h)),
        out_shape=jax.ShapeDtypeStruct((n, DV_W), BF16),
        scratch_shapes=[pltpu.VMEM((2 * tq, 1), F32), pltpu.VMEM((2 * tq, 1), F32), pltpu.VMEM((2 * tq, DIFF_V), F32)],
        compiler_params=_cparams(("parallel", "parallel", "arbitrary")),
        name="diff_attn_prompt",
    )(dq_b, dk_b, dv_b, lam_p, subln_g.reshape(1, DIFF_V))


def _mla_attn_kernel(q_ref, k_ref, wuv_ref, o_ref, m_scr, l_scr, acc_scr, *, tq, tk):
    qi = pl.program_id(1)
    q = q_ref[...].reshape(MLA_HEADS * tq, MLA_PAD_W)
    m_scr[...] = jnp.full(m_scr.shape, NEG_INF, F32)
    l_scr[...] = jnp.zeros(l_scr.shape, F32)
    acc_scr[...] = jnp.zeros(acc_scr.shape, F32)

    def block(start, masked):
        k = k_ref[pl.ds(start, tk), :]
        s = _nt(q, k)
        if masked:
            r = qi * tq + lax.broadcasted_iota(jnp.int32, s.shape, 0) % tq
            c = start + lax.broadcasted_iota(jnp.int32, s.shape, 1)
            s = jnp.where(c <= r, s, NEG_INF)
        _online_softmax_step(s, k[:, 0:KV_LORA], m_scr, l_scr, acc_scr)

    def body(j, carry):
        block(pl.multiple_of(j * tk, tk), False)
        return carry

    n_full = (qi * tq) // tk
    lax.fori_loop(0, n_full, body, 0)
    block(pl.multiple_of(n_full * tk, tk), True)

    o = (acc_scr[...] / l_scr[...]).astype(BF16)
    out = _mm(o[0:tq], wuv_ref[0])
    for hh in range(1, MLA_HEADS):
        out = out + _mm(o[hh * tq:(hh + 1) * tq], wuv_ref[hh])
    o_ref[...] = out.astype(o_ref.dtype)


def _wuv_padded(w_uv):
    eye = jnp.eye(MLA_HEADS, dtype=w_uv.dtype)
    return jnp.einsum("hcv,hg->hcgv", w_uv, eye).reshape(MLA_HEADS, KV_LORA, MLA_HEADS * V_HEAD).astype(BF16)


def _mla_attn_prompt(mq_b, mk_b, wuv_pad, batch, seq, tq, tk):
    n = mk_b.shape[0]
    nq = seq // tq
    return pl.pallas_call(
        functools.partial(_mla_attn_kernel, tq=tq, tk=tk),
        grid=(batch, nq),
        in_specs=[pl.BlockSpec((MLA_HEADS, tq, MLA_PAD_W), lambda b, i: (0, b * nq + i, 0)),
                  pl.BlockSpec((seq, MLA_PAD_W), lambda b, i: (b, 0)),
                  pl.BlockSpec((MLA_HEADS, KV_LORA, MLA_HEADS * V_HEAD), lambda b, i: (0, 0, 0))],
        out_specs=pl.BlockSpec((tq, MLA_HEADS * V_HEAD), lambda b, i: (b * nq + i, 0)),
        out_shape=jax.ShapeDtypeStruct((n, MLA_HEADS * V_HEAD), BF16),
        scratch_shapes=[pltpu.VMEM((MLA_HEADS * tq, 1), F32), pltpu.VMEM((MLA_HEADS * tq, 1), F32),
                        pltpu.VMEM((MLA_HEADS * tq, KV_LORA), F32)],
        compiler_params=_cparams(("parallel", "arbitrary")),
        name="mla_attn_prompt",
    )(mq_b, mk_b, wuv_pad)


def _out_proj_kernel(h_ref, a_ref, b_ref, wa_ref, wb_ref, o_ref):
    o_ref[...] = h_ref[...] + _mm(a_ref[...], wa_ref[...]) + _mm(b_ref[...], wb_ref[...])


def _out_proj(h, a, b, w_out, tm):
    n, d = h.shape
    ka, kb = a.shape[1], b.shape[1]
    wa = w_out[:ka].astype(BF16)
    wb = w_out[ka:].astype(BF16)
    return pl.pallas_call(
        _out_proj_kernel,
        grid=(n // tm,),
        in_specs=[pl.BlockSpec((tm, d), lambda i: (i, 0)), pl.BlockSpec((tm, ka), lambda i: (i, 0)),
                  pl.BlockSpec((tm, kb), lambda i: (i, 0)), pl.BlockSpec((ka, d), lambda i: (0, 0)),
                  pl.BlockSpec((kb, d), lambda i: (0, 0))],
        out_specs=pl.BlockSpec((tm, d), lambda i: (i, 0)),
        out_shape=jax.ShapeDtypeStruct((n, d), F32),
        compiler_params=_cparams(("parallel",)),
        name="out_proj",
    )(h, a, b, wa, wb)


def _final_norm_kernel(h_ref, g_ref, o_ref):
    o_ref[...] = _rms(h_ref[...], g_ref[...])


def _final_norm(h, g, tm):
    n, d = h.shape
    return pl.pallas_call(
        _final_norm_kernel,
        grid=(n // tm,),
        in_specs=[pl.BlockSpec((tm, d), lambda i: (i, 0)), pl.BlockSpec((1, d), lambda i: (0, 0))],
        out_specs=pl.BlockSpec((tm, d), lambda i: (i, 0)),
        out_shape=jax.ShapeDtypeStruct((n, d), F32),
        compiler_params=_cparams(("parallel",)),
        name="final_norm",
    )(h, g.reshape(1, d))


def _sort16_desc(a):
    a = list(a)
    n = len(a)
    k = 2
    while k <= n:
        j = k // 2
        while j >= 1:
            for i in range(n):
                l = i ^ j
                if l > i:
                    hi = jnp.maximum(a[i], a[l])
                    lo = jnp.minimum(a[i], a[l])
                    a[i], a[l] = (hi, lo) if (i & k) == 0 else (lo, hi)
            j //= 2
        k *= 2
    return a


def _merge_top_desc(a, b):
    n = len(a)
    c = [jnp.maximum(a[i], b[n - 1 - i]) for i in range(n)]
    j = n // 2
    while j >= 1:
        for i in range(n):
            l = i ^ j
            if l > i:
                c[i], c[l] = jnp.maximum(c[i], c[l]), jnp.minimum(c[i], c[l])
        j //= 2
    return c


def _top16_sorted(xs):
    groups = [_sort16_desc(xs[g:g + PEER_TOPK]) for g in range(0, len(xs), PEER_TOPK)]
    while len(groups) > 1:
        groups = [_merge_top_desc(groups[i], groups[i + 1]) for i in range(0, len(groups), 2)]
    return groups[0]


def _pair_threshold(sv1, sv2):
    k = PEER_TOPK
    cur = [sv1[i] + sv2[0] for i in range(k)]
    cnt = [jnp.zeros(sv1[0].shape, jnp.int32) for _ in range(k)]
    m0 = cur[0]
    z = jnp.zeros(sv1[0].shape, F32)
    m = m0
    for step in range(k):
        m = cur[0]
        for i in range(1, k):
            m = jnp.maximum(m, cur[i])
        z = z + jnp.exp(m - m0)
        if step == k - 1:
            break
        first = jnp.full(m.shape, k, jnp.int32)
        for i in range(k - 1, -1, -1):
            first = jnp.where(cur[i] == m, i, first)
        sel = [first == i for i in range(k)]
        nsel = jnp.zeros(m.shape, jnp.int32)
        for i in range(k):
            nsel = jnp.where(sel[i], cnt[i], nsel)
        nsel = nsel + 1
        v2 = jnp.full(m.shape, NEG_INF, F32)
        for r in range(1, k):
            v2 = jnp.where(nsel == r, sv2[r], v2)
        for i in range(k):
            cnt[i] = jnp.where(sel[i], nsel, cnt[i])
            cur[i] = jnp.where(sel[i], sv1[i] + v2, cur[i])
    return m, z


def _peer_route_kernel(h_ref, g_ref, wq_ref, keys_ref, xn_ref, q1_ref, a1_ref, r2_ref, b2_ref, sc_scr):
    xn = _rms(h_ref[...], g_ref[...]).astype(BF16)
    xn_ref[...] = xn
    q = _mm(xn, wq_ref[...]).astype(BF16)
    nk = PEER_KEYS
    n_chunks = sc_scr.shape[1]
    for p in range(2):
        for hh in range(PEER_HEADS):
            c0 = (p * PEER_HEADS + hh) * nk
            sc = _nt(keys_ref[p, hh], q[:, c0:c0 + nk])
            for c in range(n_chunks):
                sc_scr[p, c, hh * nk:(hh + 1) * nk, :] = sc[:, c * LANES:(c + 1) * LANES]

    def per_chunk(c, carry):
        ls = pl.ds(pl.multiple_of(c * LANES, LANES), LANES)
        x1 = [sc_scr[0, c, pl.ds(kk, PEER_HEADS, stride=nk), :] for kk in range(nk)]
        x2 = [sc_scr[1, c, pl.ds(kk, PEER_HEADS, stride=nk), :] for kk in range(nk)]
        sv1 = _top16_sorted(x1)
        sv2 = _top16_sorted(x2)
        t16, z = _pair_threshold(sv1, sv2)
        rz = 1.0 / z
        for kk in range(nk):
            cnt = jnp.zeros(t16.shape, F32)
            for k2 in range(PEER_TOPK):
                cnt = cnt + jnp.where(x1[kk] + sv2[k2] >= t16, 1.0, 0.0)
            q1_ref[kk, :, ls] = cnt
            a1_ref[kk, :, ls] = jnp.exp(x1[kk] - sv1[0]) * rz
        for hh in range(PEER_HEADS):
            s2 = sc_scr[1, c, hh * nk:(hh + 1) * nk, :]
            rank = jnp.zeros(s2.shape, F32)
            for k2 in range(PEER_TOPK):
                rank = rank + jnp.where(sv2[k2][hh:hh + 1, :] > s2, 1.0, 0.0)
            r2_ref[hh, :, ls] = rank.astype(r2_ref.dtype)
            b2_ref[hh, :, ls] = jnp.exp(s2 - sv2[0][hh:hh + 1, :]).astype(b2_ref.dtype)
        return carry

    lax.fori_loop(0, n_chunks, per_chunk, 0)


def _peer_route(h, norm_g, wq_b, keys_b, tm):
    n, d = h.shape
    nh, nk = PEER_HEADS, PEER_KEYS
    return pl.pallas_call(
        _peer_route_kernel,
        grid=(n // tm,),
        in_specs=[pl.BlockSpec((tm, d), lambda i: (i, 0)), pl.BlockSpec((1, d), lambda i: (0, 0)),
                  pl.BlockSpec((d, 2 * nh * nk), lambda i: (0, 0)),
                  pl.BlockSpec((2, nh, nk, nk), lambda i: (0, 0, 0, 0))],
        out_specs=(pl.BlockSpec((tm, d), lambda i: (i, 0)),
                   pl.BlockSpec((nk, nh, tm), lambda i: (0, 0, i)), pl.BlockSpec((nk, nh, tm), lambda i: (0, 0, i)),
                   pl.BlockSpec((nh, nk, tm), lambda i: (0, 0, i)), pl.BlockSpec((nh, nk, tm), lambda i: (0, 0, i))),
        out_shape=(jax.ShapeDtypeStruct((n, d), BF16),
                   jax.ShapeDtypeStruct((nk, nh, n), F32), jax.ShapeDtypeStruct((nk, nh, n), F32),
                   jax.ShapeDtypeStruct((nh, nk, n), BF16), jax.ShapeDtypeStruct((nh, nk, n), BF16)),
        scratch_shapes=[pltpu.VMEM((2, tm // LANES, nh * nk, LANES), F32)],
        compiler_params=_cparams(("parallel",)),
        name="peer_route",
    )(h, norm_g.reshape(1, d), wq_b, keys_b)


def _gelu_tanh(x):
    return 0.5 * x * (1.0 + jnp.tanh(0.7978845608028654 * (x + 0.044715 * (x * x * x))))


def _rows_bf16(ref, j, hh, ls, rows):
    r = jnp.broadcast_to(ref[j, hh:hh + 1, ls], (16, ls.stop - ls.start)).astype(BF16)
    return jnp.concatenate([r] * (rows // 16), axis=0)


def _peer_dense_kernel(xt_ref, u_ref, vt_ref, q1_ref, a1_ref, r2_ref, b2_ref, h_ref, o_ref, acc_scr, wt_scr, *, lc):
    ei = pl.program_id(1)
    et, tt = wt_scr.shape
    nk = PEER_KEYS

    @pl.when(ei == 0)
    def _():
        acc_scr[...] = jnp.zeros(acc_scr.shape, F32)

    act = _mm(u_ref[...], xt_ref[...]).astype(BF16)
    for j in range(et // nk):
        rs = slice(j * nk, (j + 1) * nk)
        for c in range(tt // lc):
            ls = slice(c * lc, (c + 1) * lc)
            gate = jnp.zeros((nk, lc), BF16)
            for hh in range(PEER_HEADS):
                keep = r2_ref[hh, :, ls] < _rows_bf16(q1_ref, j, hh, ls, nk)
                gate = gate + jnp.where(keep, b2_ref[hh, :, ls], jnp.zeros((), BF16)) * _rows_bf16(a1_ref, j, hh, ls, nk)
            wt_scr[rs, ls] = gate * _gelu_tanh(act[rs, ls])
    acc_scr[...] += _mm(vt_ref[...], wt_scr[...])

    @pl.when(ei == pl.num_programs(1) - 1)
    def _():
        o_ref[...] = h_ref[...] + acc_scr[...].T


def _peer_dense(h, xt, u_b, vt_b, q1, a1, r2, b2, tt, et, lc=2 * LANES):
    n, d = h.shape
    ne = u_b.shape[0]
    nh, nk = PEER_HEADS, PEER_KEYS
    return pl.pallas_call(
        functools.partial(_peer_dense_kernel, lc=lc),
        grid=(n // tt, ne // et),
        in_specs=[pl.BlockSpec((d, tt), lambda t, e: (0, t)),
                  pl.BlockSpec((et, d), lambda t, e: (e, 0)),
                  pl.BlockSpec((d, et), lambda t, e: (0, e)),
                  pl.BlockSpec((et // nk, nh, tt), lambda t, e: (e, 0, t)),
                  pl.BlockSpec((et // nk, nh, tt), lambda t, e: (e, 0, t)),
                  pl.BlockSpec((nh, nk, tt), lambda t, e: (0, 0, t)),
                  pl.BlockSpec((nh, nk, tt), lambda t, e: (0, 0, t)),
                  pl.BlockSpec((tt, d), lambda t, e: (t, 0))],
        out_specs=pl.BlockSpec((tt, d), lambda t, e: (t, 0)),
        out_shape=jax.ShapeDtypeStruct((n, d), F32),
        scratch_shapes=[pltpu.VMEM((d, tt), F32), pltpu.VMEM((et, tt), BF16)],
        compiler_params=_cparams(("parallel", "arbitrary")),
        name="peer_dense",
    )(xt, u_b, vt_b, q1, a1, r2, b2, h)


def _peer_weights(w_q, keys, u_tab, v_tab):
    d = w_q.shape[0]
    nh, nk = PEER_HEADS, PEER_KEYS
    wq_b = w_q.reshape(d, nh, 2, nk).transpose(0, 2, 1, 3).reshape(d, 2 * nh * nk).astype(BF16)
    keys_b = jnp.swapaxes(keys, 0, 1).astype(BF16)
    return wq_b, keys_b, u_tab.astype(BF16), v_tab.T.astype(BF16)


def _peer_layer(h, norm_g, pw, tm, tt, et):
    wq_b, keys_b, u_b, vt_b = pw
    xn_b, q1, a1, r2, b2 = _peer_route(h, norm_g, wq_b, keys_b, tm)
    return _peer_dense(h, xn_b.T, u_b, vt_b, q1, a1, r2, b2, tt, et)


Q_ROWS_D = 16


def _new_token_init(q, kn, vn, n_tok, m_ref, l_ref, acc_ref):
    tok = lax.broadcasted_iota(jnp.int32, (q.shape[0], 1), 0) % n_tok
    ss = []
    for j in range(n_tok):
        sj = jnp.sum(q * kn[j:j + 1, :], axis=1, keepdims=True)
        ss.append(jnp.where(tok >= j, sj, NEG_INF))
    m = ss[0]
    for sj in ss[1:]:
        m = jnp.maximum(m, sj)
    ps = [jnp.exp(sj - m) for sj in ss]
    l = ps[0]
    acc = ps[0] * vn[0:1, :]
    for j in range(1, n_tok):
        l = l + ps[j]
        acc = acc + ps[j] * vn[j:j + 1, :]
    m_ref[...] = m
    l_ref[...] = l
    acc_ref[...] = acc


def _paged_diff_kernel(pt_ref, q_ref, kn_ref, vn_ref, lp_ref, g_ref, *rest, n_pg, n_tok, lam_init):
    k_refs = rest[:n_pg]
    v_refs = rest[n_pg:2 * n_pg]
    o_ref = rest[2 * n_pg]
    m_scr, l_scr, acc_scr = rest[2 * n_pg + 1:]
    c = pl.program_id(1)

    @pl.when(c == 0)
    def _():
        kn = kn_ref[...]
        vn = vn_ref[...]
        for hh in range(DIFF_HEADS):
            sl = slice(hh * LANES, (hh + 1) * LANES)
            _new_token_init(q_ref[hh], kn[:, sl], vn[:, sl], n_tok, m_scr.at[hh], l_scr.at[hh], acc_scr.at[hh])

    for hh in range(DIFF_HEADS):
        sl = slice(hh * LANES, (hh + 1) * LANES)
        qb = q_ref[hh].astype(BF16)
        s = jnp.concatenate([_mm(qb, k_refs[j][sl, :].astype(BF16)) for j in range(n_pg)], axis=1)
        m_prev = m_scr[hh]
        m_new = jnp.maximum(m_prev, jnp.max(s, axis=1, keepdims=True))
        alpha = jnp.exp(m_prev - m_new)
        p = jnp.exp(s - m_new)
        l_scr[hh] = alpha * l_scr[hh] + jnp.sum(p, axis=1, keepdims=True)
        pb = p.astype(BF16)
        pv = None
        for j in range(n_pg):
            vh = v_refs[j][pl.ds(hh, LANES, stride=DIFF_HEADS), :].astype(BF16)
            part = _mm(pb[:, j * LANES:(j + 1) * LANES], vh)
            pv = part if pv is None else pv + part
        acc_scr[hh] = alpha * acc_scr[hh] + pv
        m_scr[hh] = m_new

    @pl.when(c == pl.num_programs(1) - 1)
    def _():
        lam = _lambda_value(lp_ref[...], lam_init)
        for hh in range(DIFF_HEADS):
            o = acc_scr[hh] / l_scr[hh]
            od = o[0:n_tok] - lam * o[n_tok:2 * n_tok]
            od = _rms(od, g_ref[...]) * (1.0 - lam_init)
            o_ref[:, hh * LANES:(hh + 1) * LANES] = od


def _paged_diff(q, kn, vn, lam_p, subln_g, cache_k, cache_v, page_table, li, lam_init, n_pg):
    bsz, n_tok, _ = kn.shape
    n_pages = page_table.shape[1]
    page = cache_k.shape[2]
    assert page == LANES and n_pages % n_pg == 0
    ck = jnp.transpose(cache_k, (0, 1, 3, 4, 2)).reshape(cache_k.shape[0], cache_k.shape[1], DQ_W, page)
    cv = cache_v.reshape(cache_v.shape[0], cache_v.shape[1], page * DIFF_HEADS, DIFF_V)

    def k_spec(j):
        return pl.BlockSpec((None, None, DQ_W, page), lambda b, c, pt: (li, pt[b * n_pages + c * n_pg + j], 0, 0))

    def v_spec(j):
        return pl.BlockSpec((None, None, page * DIFF_HEADS, DIFF_V),
                            lambda b, c, pt: (li, pt[b * n_pages + c * n_pg + j], 0, 0))

    grid_spec = pltpu.PrefetchScalarGridSpec(
        num_scalar_prefetch=1,
        grid=(bsz, n_pages // n_pg),
        in_specs=[pl.BlockSpec((None, DIFF_HEADS, Q_ROWS_D, LANES), lambda b, c, pt: (b, 0, 0, 0)),
                  pl.BlockSpec((None, n_tok, DQ_W), lambda b, c, pt: (b, 0, 0)),
                  pl.BlockSpec((None, n_tok, DV_W), lambda b, c, pt: (b, 0, 0)),
                  pl.BlockSpec((4, DIFF_D), lambda b, c, pt: (0, 0)),
                  pl.BlockSpec((1, DIFF_V), lambda b, c, pt: (0, 0))]
                 + [k_spec(j) for j in range(n_pg)] + [v_spec(j) for j in range(n_pg)],
        out_specs=pl.BlockSpec((None, n_tok, DV_W), lambda b, c, pt: (b, 0, 0)),
        scratch_shapes=[pltpu.VMEM((DIFF_HEADS, Q_ROWS_D, 1), F32), pltpu.VMEM((DIFF_HEADS, Q_ROWS_D, 1), F32),
                        pltpu.VMEM((DIFF_HEADS, Q_ROWS_D, DIFF_V), F32)],
    )
    return pl.pallas_call(
        functools.partial(_paged_diff_kernel, n_pg=n_pg, n_tok=n_tok, lam_init=lam_init),
        grid_spec=grid_spec,
        out_shape=jax.ShapeDtypeStruct((bsz, n_tok, DV_W), F32),
        compiler_params=_cparams(("parallel", "arbitrary")),
        name="paged_diff",
    )(page_table.reshape(-1), q, kn, vn, lam_p, subln_g.reshape(1, DIFF_V), *([ck] * n_pg), *([cv] * n_pg))


def _paged_mla_kernel(pt_ref, q_ref, kn_ref, *rest, n_pg, n_tok):
    pg_refs = rest[:n_pg]
    o_ref = rest[n_pg]
    m_scr, l_scr, acc_scr = rest[n_pg + 1:]
    c = pl.program_id(1)
    q = q_ref[...]

    @pl.when(c == 0)
    def _():
        kn = kn_ref[...]
        _new_token_init(q, kn, kn[:, 0:KV_LORA], n_tok, m_scr, l_scr, acc_scr)

    qb = q.astype(BF16)
    q_lat = qb[:, 0:KV_LORA]
    q_rot = qb[:, KV_LORA:KV_LORA + LANES]
    kts = [pg_refs[j][...].astype(BF16) for j in range(n_pg)]
    s = jnp.concatenate([_mm(q_lat, kt[0:KV_LORA]) + _mm(q_rot[:, 0:QK_ROPE], kt[KV_LORA:MLA_CACHE_W]) for kt in kts], axis=1)
    m_prev = m_scr[...]
    m_new = jnp.maximum(m_prev, jnp.max(s, axis=1, keepdims=True))
    alpha = jnp.exp(m_prev - m_new)
    p = jnp.exp(s - m_new)
    l_scr[...] = alpha * l_scr[...] + jnp.sum(p, axis=1, keepdims=True)
    pb = p.astype(BF16)
    pv = _nt(pb[:, 0:LANES], kts[0][0:KV_LORA])
    for j in range(1, n_pg):
        pv = pv + _nt(pb[:, j * LANES:(j + 1) * LANES], kts[j][0:KV_LORA])
    acc_scr[...] = alpha * acc_scr[...] + pv
    m_scr[...] = m_new

    @pl.when(c == pl.num_programs(1) - 1)
    def _():
        o_ref[...] = acc_scr[...] / l_scr[...]


def _paged_mla(q, kn, cache_m, page_table, li, n_pg):
    bsz, n_tok, _ = kn.shape
    rows = q.shape[1]
    n_pages = page_table.shape[1]
    page = cache_m.shape[2]
    assert page == LANES and n_pages % n_pg == 0

    cache_t = jnp.swapaxes(cache_m, 2, 3)

    def page_spec(j):
        return pl.BlockSpec((None, None, MLA_CACHE_W, page), lambda b, c, pt: (li, pt[b * n_pages + c * n_pg + j], 0, 0))

    grid_spec = pltpu.PrefetchScalarGridSpec(
        num_scalar_prefetch=1,
        grid=(bsz, n_pages // n_pg),
        in_specs=[pl.BlockSpec((None, rows, MLA_PAD_W), lambda b, c, pt: (b, 0, 0)),
                  pl.BlockSpec((None, n_tok, MLA_PAD_W), lambda b, c, pt: (b, 0, 0))]
                 + [page_spec(j) for j in range(n_pg)],
        out_specs=pl.BlockSpec((None, rows, KV_LORA), lambda b, c, pt: (b, 0, 0)),
        scratch_shapes=[pltpu.VMEM((rows, 1), F32), pltpu.VMEM((rows, 1), F32), pltpu.VMEM((rows, KV_LORA), F32)],
    )
    return pl.pallas_call(
        functools.partial(_paged_mla_kernel, n_pg=n_pg, n_tok=n_tok),
        grid_spec=grid_spec,
        out_shape=jax.ShapeDtypeStruct((bsz, rows, KV_LORA), F32),
        compiler_params=_cparams(("parallel", "arbitrary")),
        name="paged_mla",
    )(page_table.reshape(-1), q, kn, *([cache_t] * n_pg))


def _head_proj_kernel(o_ref, wuv_ref, out_ref):
    out = _mm(o_ref[0], wuv_ref[0])
    for hh in range(1, MLA_HEADS):
        out = out + _mm(o_ref[hh], wuv_ref[hh])
    out_ref[...] = out.astype(out_ref.dtype)


def _head_proj(o_h, wuv_pad):
    _, n, _ = o_h.shape
    return pl.pallas_call(
        _head_proj_kernel,
        out_shape=jax.ShapeDtypeStruct((n, MLA_HEADS * V_HEAD), BF16),
        compiler_params=_cparams(()),
        name="head_proj",
    )(o_h, wuv_pad)


def _attn_sample_layer(h, bsz, n_tok, past_len, lam_init, prm, caches, li, n_pg):
    (norm_g, w_in, lam_p, subln_g, qn_g, kvn_g, w_uq, w_uk, w_uv, w_out) = prm
    cache_k, cache_v, cache_m, page_table = caches
    n = bsz * n_tok
    win, wuq, wuk, sel = _attn_in_weights(w_in, w_uq, w_uk)
    tabs = _rope_tables(past_len + jnp.arange(n) % n_tok)
    dq_b, dk, dk_b, dv, dv_b, mq_b, mkv, mk_b = _attn_in(h, tabs, 1, norm_g, win, qn_g, kvn_g, wuq, wuk, sel, n)
    q4 = dq_b.astype(F32).reshape(bsz, n_tok, DIFF_HEADS, 2, DIFF_D)
    eye = jnp.eye(2, dtype=F32)
    qd = jnp.einsum("bthmd,mk->bhmtkd", q4, eye).reshape(bsz, DIFF_HEADS, 2 * n_tok, 2 * DIFF_D)
    qd = jnp.pad(qd, ((0, 0), (0, 0), (0, Q_ROWS_D - 2 * n_tok), (0, 0)))
    od = _paged_diff(qd, dk_b.astype(F32).reshape(bsz, n_tok, DQ_W), dv_b.astype(F32).reshape(bsz, n_tok, DV_W),
                     lam_p, subln_g, cache_k, cache_v, page_table, li, lam_init, n_pg)
    qm = mq_b.astype(F32).reshape(MLA_HEADS, bsz, n_tok, MLA_PAD_W).transpose(1, 0, 2, 3).reshape(bsz, MLA_HEADS * n_tok, MLA_PAD_W)
    om = _paged_mla(qm, mk_b.astype(F32).reshape(bsz, n_tok, MLA_PAD_W), cache_m, page_table, li, n_pg)
    om = om.reshape(bsz, MLA_HEADS, n_tok, KV_LORA).transpose(1, 0, 2, 3).reshape(MLA_HEADS, n, KV_LORA).astype(BF16)
    om = _head_proj(om, _wuv_padded(w_uv))
    h = _out_proj(h, od.reshape(n, DV_W).astype(BF16), om, w_out, n)
    return h, dk, dv, mkv


def _sigmoid(x):
    return 1.0 / (1.0 + jnp.exp(-x))


def _mix_in_kernel(h_ref, g_ref, win_ref, u_ref, rkv_ref, lora_ref):
    xn = _rms(h_ref[...], g_ref[...]).astype(BF16)
    z = _mm(xn, win_ref[...])
    u_ref[...] = z[:, 0:CONV_CH] * _sigmoid(z[:, CONV_CH:2 * CONV_CH])
    rkv_ref[...] = z[:, 2 * CONV_CH:2 * CONV_CH + 3 * RWKV_W]
    lora_ref[...] = z[:, 2 * CONV_CH + 3 * RWKV_W:MIX_IN_PAD]


def _mix_in(h, norm_g, win, tm):
    n, d = h.shape
    row = lambda w: pl.BlockSpec((tm, w), lambda i: (i, 0))
    return pl.pallas_call(
        _mix_in_kernel,
        grid=(n // tm,),
        in_specs=[row(d), pl.BlockSpec((1, d), lambda i: (0, 0)), pl.BlockSpec((d, MIX_IN_PAD), lambda i: (0, 0))],
        out_specs=(row(CONV_CH), row(3 * RWKV_W), row(LORA_PAD)),
        out_shape=(jax.ShapeDtypeStruct((n, CONV_CH), F32), jax.ShapeDtypeStruct((n, 3 * RWKV_W), F32),
                   jax.ShapeDtypeStruct((n, LORA_PAD), F32)),
        compiler_params=_cparams(("parallel",)),
        name="mix_in",
    )(h, norm_g.reshape(1, d), win)


CONV_PAD_ROWS = 32


def _conv_kernel(u_ref, st_ref, cw_ref, cb_ref, lg_ref, lb_ref, c_ref, nb_ref, full_scr, *, tt):
    t_len = u_ref.shape[0]
    full_scr[0:CONV_PAD_ROWS, :] = st_ref[...]
    full_scr[CONV_PAD_ROWS:CONV_PAD_ROWS + t_len, :] = u_ref[...]
    lead = CONV_PAD_ROWS - (CONV_W - 1)

    def chunk(ci, carry):
        t0 = ci * tt if isinstance(ci, int) else pl.multiple_of(ci * tt, tt)
        acc = jnp.zeros((tt, CONV_CH), F32) + cb_ref[...]
        win = full_scr[pl.ds(t0, tt + CONV_PAD_ROWS), :]
        for w in range(CONV_W):
            acc = acc + win[w + lead:w + lead + tt, :] * cw_ref[w:w + 1, :]
        mu = jnp.mean(acc, axis=-1, keepdims=True)
        xc = acc - mu
        var = jnp.mean(xc * xc, axis=-1, keepdims=True)
        y = xc * lax.rsqrt(var + LN_EPS) * lg_ref[...] + lb_ref[...]
        c_ref[pl.ds(t0, tt), :] = (y * _sigmoid(y)).astype(c_ref.dtype)
        return carry

    if t_len == tt:
        chunk(0, 0)
    else:
        lax.fori_loop(0, t_len // tt, chunk, 0)
    nb_ref[...] = full_scr[t_len:t_len + CONV_PAD_ROWS, :]


def _conv(u3, conv_buf, cw, cb, lng, lnb, tt):
    bsz, t_len, ch = u3.shape
    st = jnp.pad(conv_buf, ((0, 0), (CONV_PAD_ROWS - (CONV_W - 1), 0), (0, 0)))
    vec = lambda: pl.BlockSpec((1, ch), lambda b: (0, 0))
    c_b, nb = pl.pallas_call(
        functools.partial(_conv_kernel, tt=tt),
        grid=(bsz,),
        in_specs=[pl.BlockSpec((None, t_len, ch), lambda b: (b, 0, 0)),
                  pl.BlockSpec((None, CONV_PAD_ROWS, ch), lambda b: (b, 0, 0)),
                  pl.BlockSpec((CONV_W, ch), lambda b: (0, 0)), vec(), vec(), vec()],
        out_specs=(pl.BlockSpec((None, t_len, ch), lambda b: (b, 0, 0)),
                   pl.BlockSpec((None, CONV_PAD_ROWS, ch), lambda b: (b, 0, 0))),
        out_shape=(jax.ShapeDtypeStruct((bsz, t_len, ch), BF16), jax.ShapeDtypeStruct((bsz, CONV_PAD_ROWS, ch), F32)),
        scratch_shapes=[pltpu.VMEM((t_len + CONV_PAD_ROWS, ch), F32)],
        compiler_params=_cparams(("parallel",)),
        name="conv",
    )(u3, st, cw, cb.reshape(1, ch), lng.reshape(1, ch), lnb.reshape(1, ch))
    return c_b, nb[:, CONV_PAD_ROWS - (CONV_W - 1):, :]


def _head_sum(x, ones_ref):
    hi = x.astype(BF16)
    lo = (x - hi.astype(F32)).astype(BF16)
    return _mm(hi, ones_ref[...]) + _mm(lo, ones_ref[...])


def _rwkv_prep_kernel(rkv_ref, prkv_ref, lora_ref, plora_ref, mu_ref, mul_ref, w0_ref, w2_ref, a0_ref, a2_ref, g2_ref,
                      kk_ref_w, ka_ref, ones_ref, r_ref, w_ref, k_ref, v_ref, kk_ref, a_ref, g_ref):
    rkv = rkv_ref[...]
    zs = rkv + (prkv_ref[...] - rkv) * mu_ref[...]
    lora = lora_ref[...]
    ls = lora + (plora_ref[...] - lora) * mul_ref[...]
    r = zs[:, 0:RWKV_W]
    k = zs[:, RWKV_W:2 * RWKV_W]
    v = zs[:, 2 * RWKV_W:3 * RWKV_W]
    wdec = _mm(jnp.tanh(ls).astype(BF16), w2_ref[...])
    x = -(w0_ref[...] + wdec)
    softplus = jnp.maximum(x, 0.0) + jnp.log(1.0 + jnp.exp(-jnp.abs(x)))
    wlog = -softplus - 0.5
    w_ref[...] = jnp.exp(-jnp.exp(wlog))
    a = _sigmoid(a0_ref[...] + _mm(ls.astype(BF16), a2_ref[...]))
    g_ref[...] = _mm(_sigmoid(ls).astype(BF16), g2_ref[...])
    kk = k * kk_ref_w[...]
    nrm = jnp.sqrt(_head_sum(kk * kk, ones_ref))
    kk_ref[...] = kk / jnp.maximum(nrm, 1e-12)
    k_ref[...] = k * (1.0 + (a - 1.0) * ka_ref[...])
    r_ref[...] = r
    v_ref[...] = v
    a_ref[...] = a


def _lora_pad(w, row0):
    out = jnp.zeros((LORA_PAD, w.shape[1]), F32)
    return lax.dynamic_update_slice(out, w.astype(F32), (row0, 0)).astype(BF16)


def _head_ones():
    idx = np.arange(RWKV_W) // RWKV_N
    return jnp.asarray((idx[:, None] == idx[None, :]).astype(np.float32), BF16)


def _rwkv_prep(rkv, prkv, lora, plora, mu, w0, w2, a0, a2, g2, k_k, k_a, tm):
    n = rkv.shape[0]
    row = lambda w: pl.BlockSpec((tm, w), lambda i: (i, 0))
    full = lambda a, b: pl.BlockSpec((a, b), lambda i: (0, 0))
    mu_rkv = mu[:3 * RWKV_W].reshape(1, -1)
    mu_l = jnp.pad(mu[3 * RWKV_W:], (0, LORA_PAD - LORA_W)).reshape(1, -1)
    o = jax.ShapeDtypeStruct((n, RWKV_W), F32)
    return pl.pallas_call(
        _rwkv_prep_kernel,
        grid=(n // tm,),
        in_specs=[row(3 * RWKV_W), row(3 * RWKV_W), row(LORA_PAD), row(LORA_PAD), full(1, 3 * RWKV_W), full(1, LORA_PAD),
                  full(1, RWKV_W), full(LORA_PAD, RWKV_W), full(1, RWKV_W), full(LORA_PAD, RWKV_W), full(LORA_PAD, RWKV_W),
                  full(1, RWKV_W), full(1, RWKV_W), full(RWKV_W, RWKV_W)],
        out_specs=(row(RWKV_W),) * 7,
        out_shape=(o,) * 7,
        compiler_params=_cparams(("parallel",)),
        name="rwkv_prep",
    )(rkv, prkv, lora, plora, mu_rkv, mu_l, w0.reshape(1, -1), _lora_pad(w2, 0), a0.reshape(1, -1), _lora_pad(a2, 32),
      _lora_pad(g2, 64), k_k.reshape(1, -1), k_a.reshape(1, -1), _head_ones())


def _wkv_scan_kernel(r_ref, w_ref, k_ref, v_ref, kk_ref, a_ref, s0_ref, o_ref, s_out_ref, s_scr):
    ti = pl.program_id(1)
    tc = r_ref.shape[0]
    nn = RWKV_N

    @pl.when(ti == 0)
    def _():
        s_scr[...] = s0_ref[...]

    def step(t, carry):
        kk = kk_ref[t]
        w = w_ref[t]
        k = k_ref[t]
        r = r_ref[t]
        v = v_ref[t]
        kka = kk * a_ref[t]
        sk = s_scr[0] * kk[0:1, :]
        for j in range(1, nn):
            sk = sk + s_scr[j] * kk[j:j + 1, :]
        y = jnp.zeros(sk.shape, F32)
        for j in range(nn):
            s_new = s_scr[j] * w[j:j + 1, :] - sk * kka[j:j + 1, :] + v * k[j:j + 1, :]
            s_scr[j] = s_new
            y = y + s_new * r[j:j + 1, :]
        o_ref[t] = y
        return carry

    lax.fori_loop(0, tc, step, 0)

    @pl.when(ti == pl.num_programs(1) - 1)
    def _():
        s_out_ref[...] = s_scr[...]


def _wkv_scan(r, w, k, v, kk, a, s0, tc):
    t_len, nn, lanes = r.shape
    seq = pl.BlockSpec((tc, nn, LANES), lambda l, t: (t, 0, l))
    st = pl.BlockSpec((nn, nn, LANES), lambda l, t: (0, 0, l))
    return pl.pallas_call(
        _wkv_scan_kernel,
        grid=(lanes // LANES, t_len // tc),
        in_specs=[seq] * 6 + [st],
        out_specs=(seq, st),
        out_shape=(jax.ShapeDtypeStruct((t_len, nn, lanes), F32), jax.ShapeDtypeStruct((nn, nn, lanes), F32)),
        scratch_shapes=[pltpu.VMEM((nn, nn, LANES), F32)],
        compiler_params=_cparams(("parallel", "arbitrary")),
        name="wkv_scan",
    )(r, w, k, v, kk, a, s0)


def _rwkv_post_kernel(o_ref, r_ref, k_ref, v_ref, g_ref, gg_ref, gb_ref, rk_ref, ones_ref, d_ref):
    o = o_ref[...]
    inv_n = 1.0 / RWKV_N
    mean = _head_sum(o, ones_ref) * inv_n
    xc = o - mean
    var = _head_sum(xc * xc, ones_ref) * inv_n
    on = xc * lax.rsqrt(var + GN_EPS) * gg_ref[...] + gb_ref[...]
    bonus = _head_sum(r_ref[...] * k_ref[...] * rk_ref[...], ones_ref) * v_ref[...]
    d_ref[...] = ((on + bonus) * g_ref[...]).astype(d_ref.dtype)


def _rwkv_post(o, r, k, v, g, gn_g, gn_b, r_k, tm):
    n = o.shape[0]
    row = pl.BlockSpec((tm, RWKV_W), lambda i: (i, 0))
    vec = pl.BlockSpec((1, RWKV_W), lambda i: (0, 0))
    return pl.pallas_call(
        _rwkv_post_kernel,
        grid=(n // tm,),
        in_specs=[row] * 5 + [vec] * 3 + [pl.BlockSpec((RWKV_W, RWKV_W), lambda i: (0, 0))],
        out_specs=row,
        out_shape=jax.ShapeDtypeStruct((n, RWKV_W), BF16),
        compiler_params=_cparams(("parallel",)),
        name="rwkv_post",
    )(o, r, k, v, g, gn_g.reshape(1, -1), gn_b.reshape(1, -1), r_k.reshape(1, -1), _head_ones())


def _mix_layer(h, bsz, t_len, prm, conv_buf, shift_prev, wkv0, tm, tt, tc):
    (norm_g, w_in, cw, cb, lng, lnb, mu, w0, w2, a0, a2, g2, k_k, k_a, r_k, gn_g, gn_b, w_out) = prm
    n, d = h.shape
    mix_in_w = 2 * CONV_CH + 3 * RWKV_W + LORA_W
    win = jnp.pad(w_in, ((0, 0), (0, MIX_IN_PAD - mix_in_w))).astype(BF16)
    u, rkv, lora = _mix_in(h, norm_g, win, tm)
    c_b, new_buf = _conv(u.reshape(bsz, t_len, CONV_CH), conv_buf, cw, cb, lng, lnb, tt)
    rkv3 = rkv.reshape(bsz, t_len, 3 * RWKV_W)
    lora3 = lora.reshape(bsz, t_len, LORA_PAD)
    sp_l = jnp.pad(shift_prev[:, 3 * RWKV_W:], ((0, 0), (0, LORA_PAD - LORA_W)))
    prkv = jnp.concatenate([shift_prev[:, None, :3 * RWKV_W], rkv3[:, :-1]], axis=1).reshape(n, -1)
    plora = jnp.concatenate([sp_l[:, None, :], lora3[:, :-1]], axis=1).reshape(n, -1)
    r, w, k, v, kk, a, g = _rwkv_prep(rkv, prkv, lora, plora, mu, w0, w2, a0, a2, g2, k_k, k_a, tm)

    def lanes_major(x):
        return x.reshape(bsz, t_len, RWKV_HEADS, RWKV_N).transpose(1, 3, 0, 2).reshape(t_len, RWKV_N, bsz * RWKV_HEADS)

    s0 = wkv0.transpose(3, 2, 0, 1).reshape(RWKV_N, RWKV_N, bsz * RWKV_HEADS)
    o_l, s_l = _wkv_scan(*(lanes_major(x) for x in (r, w, k, v, kk, a)), s0, tc)
    o = o_l.reshape(t_len, RWKV_N, bsz, RWKV_HEADS).transpose(2, 0, 3, 1).reshape(n, RWKV_W)
    d_b = _rwkv_post(o, r, k, v, g, gn_g, gn_b, r_k.reshape(-1), tm)
    h = _out_proj(h, c_b.reshape(n, CONV_CH), d_b, w_out, tm)
    new_shift = jnp.concatenate([rkv3[:, -1], lora3[:, -1, :LORA_W]], axis=-1)
    wkv = s_l.reshape(RWKV_N, RWKV_N, bsz, RWKV_HEADS).transpose(2, 3, 1, 0)
    return h, new_buf, new_shift, wkv


def _attn_prompt_layer(h, batch, seq, lam_init, prm, tm, tq_d, tk_d, tq_m, tk_m):
    (norm_g, w_in, lam_p, subln_g, qn_g, kvn_g, w_uq, w_uk, w_uv, w_out) = prm
    win, wuq, wuk, sel = _attn_in_weights(w_in, w_uq, w_uk)
    tabs = _rope_tables(jnp.arange(seq))
    dq_b, dk, dk_b, dv, dv_b, mq_b, mkv, mk_b = _attn_in(h, tabs, seq // tm, norm_g, win, qn_g, kvn_g, wuq, wuk, sel, tm)
    od = _diff_attn_prompt(dq_b, dk_b, dv_b, lam_p, subln_g, batch, seq, lam_init, tq_d, tk_d)
    om = _mla_attn_prompt(mq_b, mk_b, _wuv_padded(w_uv), batch, seq, tq_m, tk_m)
    h = _out_proj(h, od, om, w_out, tm)
    return h, dk, dv, mkv


TM_PROJ = 256
TQ_DIFF = 256
TK_DIFF = 2048
TQ_MLA = 256
TK_MLA = 1024
TM_ROUTE = 256
TT_DENSE = 512
ET_DENSE = 2048
TT_CONV = 256
TC_SCAN = 32
PAGES_PER_STEP = 16


def kernel(x_prompt, x_sample, cache_diff_k, cache_diff_v, cache_mla, page_table, state_conv, state_shift, state_wkv, attn_norm, attn_w_in, diff_lambda, diff_subln, mla_q_norm, mla_kv_norm, mla_w_uq, mla_w_uk, mla_w_uv, attn_w_out, mix_norm, mix_w_in, conv_w, conv_b, conv_ln_g, conv_ln_b, rwkv_mu, rwkv_w0, rwkv_w2, rwkv_a0, rwkv_a2, rwkv_g2, rwkv_k_k, rwkv_k_a, rwkv_r_k, rwkv_ln_g, rwkv_ln_b, mix_w_out, ffn_norm, peer_w_q, peer_keys, peer_u, peer_v, final_norm):
    bp, sp, d = x_prompt.shape
    bs, ss, _ = x_sample.shape
    n_p, n_s = bp * sp, bs * ss
    past_len = page_table.shape[1] * cache_mla.shape[2]
    depth = ffn_norm.shape[0]
    h_p = x_prompt.reshape(n_p, d)
    h_s = x_sample.reshape(n_s, d)
    outs = {k: [] for k in ("dk_p", "dv_p", "mla_p", "conv_p", "shift_p", "wkv_p", "dk_s", "dv_s", "mla_s", "conv_s", "shift_s", "wkv_s")}
    for layer in range(depth):
        i = layer // 2
        if layer % 2 == 0:
            lam_init = 0.8 - 0.6 * math.exp(-0.3 * layer)
            prm = (attn_norm[i], attn_w_in[i], diff_lambda[i], diff_subln[i], mla_q_norm[i], mla_kv_norm[i],
                   mla_w_uq[i], mla_w_uk[i], mla_w_uv[i], attn_w_out[i])
            h_p, k_n, v_n, c_n = _attn_prompt_layer(h_p, bp, sp, lam_init, prm, TM_PROJ, TQ_DIFF, min(TK_DIFF, sp), TQ_MLA, min(TK_MLA, sp))
            outs["dk_p"].append(k_n.reshape(bp, sp, 2 * DIFF_HEADS, DIFF_D))
            outs["dv_p"].append(v_n.reshape(bp, sp, DIFF_HEADS, DIFF_V))
            outs["mla_p"].append(c_n.reshape(bp, sp, MLA_CACHE_W))
            h_s, k_n, v_n, c_n = _attn_sample_layer(h_s, bs, ss, past_len, lam_init, prm,
                                                    (cache_diff_k, cache_diff_v, cache_mla, page_table), i, PAGES_PER_STEP)
            outs["dk_s"].append(k_n.reshape(bs, ss, 2 * DIFF_HEADS, DIFF_D))
            outs["dv_s"].append(v_n.reshape(bs, ss, DIFF_HEADS, DIFF_V))
            outs["mla_s"].append(c_n.reshape(bs, ss, MLA_CACHE_W))
        else:
            prm = (mix_norm[i], mix_w_in[i], conv_w[i], conv_b[i], conv_ln_g[i], conv_ln_b[i], rwkv_mu[i], rwkv_w0[i],
                   rwkv_w2[i], rwkv_a0[i], rwkv_a2[i], rwkv_g2[i], rwkv_k_k[i], rwkv_k_a[i], rwkv_r_k[i], rwkv_ln_g[i],
                   rwkv_ln_b[i], mix_w_out[i])
            h_p, cb_n, sh_n, st_n = _mix_layer(
                h_p, bp, sp, prm, jnp.zeros((bp, CONV_W - 1, CONV_CH), F32), jnp.zeros((bp, 3 * RWKV_W + LORA_W), F32),
                jnp.zeros((bp, RWKV_HEADS, RWKV_N, RWKV_N), F32), TM_PROJ, TT_CONV, TC_SCAN)
            outs["conv_p"].append(cb_n)
            outs["shift_p"].append(sh_n)
            outs["wkv_p"].append(st_n)
            h_s, cb_n, sh_n, st_n = _mix_layer(h_s, bs, ss, prm, state_conv[i], state_shift[i], state_wkv[i], TM_PROJ, ss, ss)
            outs["conv_s"].append(cb_n)
            outs["shift_s"].append(sh_n)
            outs["wkv_s"].append(st_n)
        pw = _peer_weights(peer_w_q[layer], peer_keys[layer], peer_u[layer], peer_v[layer])
        h_p = _peer_layer(h_p, ffn_norm[layer], pw, TM_ROUTE, TT_DENSE, ET_DENSE)
        h_s = _peer_layer(h_s, ffn_norm[layer], pw, TM_ROUTE, TT_DENSE, ET_DENSE)
    y_p = _final_norm(h_p, final_norm, TM_PROJ).reshape(bp, sp, d)
    y_s = _final_norm(h_s, final_norm, TM_PROJ).reshape(bs, ss, d)
    st = lambda name: jnp.stack(outs[name])
    return (y_p, y_s, st("dk_p"), st("dv_p"), st("mla_p"), st("conv_p"), st("shift_p"), st("wkv_p"),
            st("dk_s"), st("dv_s"), st("mla_s"), st("conv_s"), st("shift_s"), st("wkv_s"))
```
